```python
import math
import jax, jax.numpy as jnp
from jax import lax
import numpy as np

D_MODEL = 1024
BATCH = 8
SEQ = 4096
DEPTH = 1

CHUNK = 64
Q_BLOCK = 128
N_DIFF_HEADS = 4
DIFF_HEAD_DIM = 64
ATTN_WIDTH = N_DIFF_HEADS * 2 * DIFF_HEAD_DIM
SSM_WIDTH = D_MODEL - ATTN_WIDTH
SSM_GROUP_CH = 16
SSM_GROUPS = SSM_WIDTH // SSM_GROUP_CH
SSM_STATE = 64
MIX_WIDTH = ATTN_WIDTH + SSM_WIDTH
IN_PROJ_WIDTH = 3 * ATTN_WIDTH + SSM_WIDTH
N_EXPERT_GROUPS = 4
EXPERTS_PER_GROUP = 8
N_EXPERTS = N_EXPERT_GROUPS * EXPERTS_PER_GROUP
INNER_TOP_K = 2
D_EXPERT = D_MODEL // 2
MOE_BLOCK = 128
LN_EPS = 1e-5
RMS_EPS = 1e-5

kernel_name = "hymba_style_diffattn_s5_hiermoe_deepnorm"


def _layernorm(x, g, b):
    xf = x.astype(jnp.float32)
    mu = jnp.mean(xf, axis=-1, keepdims=True)
    var = jnp.mean(jnp.square(xf - mu), axis=-1, keepdims=True)
    y = (xf - mu) * lax.rsqrt(var + LN_EPS) * g.astype(jnp.float32) + b.astype(jnp.float32)
    return y.astype(x.dtype)


def _lambda_init(layer_idx):
    return 0.8 - 0.6 * math.exp(-0.3 * layer_idx)


def _diff_attention(q, k, v, lam, lam_init, subln_g):
    b_, s_ = q.shape[0], q.shape[1]
    nb = s_ // Q_BLOCK
    scale = DIFF_HEAD_DIM ** -0.5
    qb = q.reshape(b_, nb, Q_BLOCK, N_DIFF_HEADS, 2, DIFF_HEAD_DIM).swapaxes(0, 1)
    k_chunk = jnp.arange(s_) // CHUNK

    def block(args):
        qi, bi = args
        sc = jnp.einsum('bqhcd,bkhcd->bhcqk', qi, k).astype(jnp.float32) * scale
        q_chunk = (bi * Q_BLOCK + jnp.arange(Q_BLOCK)) // CHUNK
        mask = k_chunk[None, :] <= q_chunk[:, None]
        sc = jnp.where(mask, sc, -jnp.inf)
        p = jax.nn.softmax(sc, axis=-1)
        attn = p[:, :, 0] - lam * p[:, :, 1]
        return jnp.einsum('bhqk,bkhe->bqhe', attn.astype(v.dtype), v)

    o = lax.map(block, (qb, jnp.arange(nb)))
    o = o.swapaxes(0, 1).reshape(b_, s_, N_DIFF_HEADS, 2 * DIFF_HEAD_DIM).astype(jnp.float32)
    o = o * lax.rsqrt(jnp.mean(jnp.square(o), axis=-1, keepdims=True) + RMS_EPS)
    o = o * subln_g.astype(jnp.float32) * (1.0 - lam_init)
    return o.reshape(b_, s_, ATTN_WIDTH).astype(q.dtype)


def _scan_op(e1, e2):
    a1, b1 = e1
    a2, b2 = e2
    return a1 * a2, a2 * b1 + b2


def _s5_mixer(u, a_re, a_im, log_dt, b_re, b_im, c_re, c_im, d, w_glu, b_glu):
    b_, s_ = u.shape[0], u.shape[1]
    nc = s_ // CHUNK
    f32 = jnp.float32
    uf = u.astype(f32).reshape(b_, nc, CHUNK, SSM_GROUPS, SSM_GROUP_CH).swapaxes(0, 1)
    lam = lax.complex(a_re.astype(f32), a_im.astype(f32))
    dt = jnp.exp(log_dt.astype(f32))[:, None]
    lam_bar = jnp.exp(lam * dt)
    b_bar = ((lam_bar - 1.0) / lam)[..., None] * lax.complex(b_re.astype(f32), b_im.astype(f32))
    c = lax.complex(c_re.astype(f32), c_im.astype(f32))
    dd = d.astype(f32)

    def step(h_prev, u_c):
        bu = jnp.einsum('gpc,bsgc->bsgp', b_bar, u_c.astype(jnp.complex64))
        a = jnp.broadcast_to(lam_bar, bu.shape)
        a_cum, h_loc = lax.associative_scan(_scan_op, (a, bu), axis=1)
        h = a_cum * h_prev[:, None] + h_loc
        y = jnp.real(jnp.einsum('gcp,bsgp->bsgc', c, h)) + dd * u_c
        return h[:, -1], y

    h0 = jnp.zeros((b_, SSM_GROUPS, SSM_STATE), jnp.complex64)
    _, y = lax.scan(step, h0, uf)
    y = y.swapaxes(0, 1).reshape(b_, s_, SSM_WIDTH)
    z = jax.nn.gelu(y)
    z = z * jax.nn.sigmoid(z @ w_glu.astype(f32) + b_glu.astype(f32))
    return z.astype(u.dtype)


def _hier_moe(h2d, w_rg, w_re, w_gate, w_up, w_down):
    t = h2d.shape[0]
    tk = t * INNER_TOP_K
    g_prob = jax.nn.softmax((h2d @ w_rg).astype(jnp.float32), axis=-1)
    g_p, g_idx = lax.top_k(g_prob, 1)
    e_logits = (h2d @ w_re).astype(jnp.float32).reshape(t, N_EXPERT_GROUPS, EXPERTS_PER_GROUP)
    e_logits = jnp.take_along_axis(e_logits, g_idx[:, :, None], axis=1)[:, 0]
    e_prob = jax.nn.softmax(e_logits, axis=-1)
    p2, i2 = lax.top_k(e_prob, INNER_TOP_K)
    p2 = p2 / jnp.sum(p2, axis=-1, keepdims=True)
    expert = g_idx * EXPERTS_PER_GROUP + i2
    weight = g_p * p2

    flat_e = expert.reshape(-1)
    flat_tok = jnp.repeat(jnp.arange(t, dtype=jnp.int32), INNER_TOP_K)
    flat_w = weight.reshape(-1)
    order = jnp.argsort(flat_e, stable=True)
    se = flat_e[order]
    counts = jnp.bincount(flat_e, length=N_EXPERTS)
    starts = jnp.cumsum(counts) - counts
    pcounts = (counts + MOE_BLOCK - 1) // MOE_BLOCK * MOE_BLOCK
    pends = jnp.cumsum(pcounts)
    pstarts = pends - pcounts
    dest = pstarts[se] + jnp.arange(tk) - starts[se]
    n_rows = tk + N_EXPERTS * MOE_BLOCK
    n_blocks = n_rows // MOE_BLOCK
    row_tok = jnp.zeros((n_rows,), jnp.int32).at[dest].set(flat_tok[order])
    row_w = jnp.zeros((n_rows,), jnp.float32).at[dest].set(flat_w[order])
    block_e = jnp.minimum(jnp.searchsorted(pends, jnp.arange(n_blocks) * MOE_BLOCK, side='right'),
                          N_EXPERTS - 1)

    def block_fn(args):
        tok, w, e = args
        xb = h2d[tok]
        hid = jax.nn.silu(xb @ w_gate[e]) * (xb @ w_up[e])
        yb = hid @ w_down[e]
        return yb * w[:, None].astype(yb.dtype)

    ys = lax.map(block_fn, (row_tok.reshape(n_blocks, MOE_BLOCK),
                            row_w.reshape(n_blocks, MOE_BLOCK), block_e))
    out = jnp.zeros_like(h2d).at[row_tok].add(ys.reshape(n_rows, -1).astype(h2d.dtype))
    return out


def setup_inputs(seed: int = 0) -> dict:
    key = jax.random.key(seed)
    ks = jax.random.split(key, 32)
    L = DEPTH
    beta = (8.0 * DEPTH) ** -0.25
    f32 = jnp.float32

    def nrm(k, shape, scale):
        return jax.random.normal(k, shape, f32) * scale

    a_im_base = jnp.pi * jnp.arange(SSM_STATE, dtype=f32)
    return {
        "x": nrm(ks[0], (BATCH, SEQ, D_MODEL), 1.0),
        "w_in": nrm(ks[1], (L, D_MODEL, IN_PROJ_WIDTH), D_MODEL ** -0.5),
        "lam_q1": nrm(ks[2], (L, DIFF_HEAD_DIM), 0.1),
        "lam_k1": nrm(ks[3], (L, DIFF_HEAD_DIM), 0.1),
        "lam_q2": nrm(ks[4], (L, DIFF_HEAD_DIM), 0.1),
        "lam_k2": nrm(ks[5], (L, DIFF_HEAD_DIM), 0.1),
        "subln_g": 1.0 + nrm(ks[6], (L, 2 * DIFF_HEAD_DIM), 0.02),
        "ssm_a_re": -0.5 + nrm(ks[7], (L, SSM_GROUPS, SSM_STATE), 0.01),
        "ssm_a_im": a_im_base + nrm(ks[8], (L, SSM_GROUPS, SSM_STATE), 0.01),
        "ssm_log_dt": jax.random.uniform(ks[9], (L, SSM_GROUPS), f32, math.log(1e-3), math.log(1e-1)),
        "ssm_b_re": nrm(ks[10], (L, SSM_GROUPS, SSM_STATE, SSM_GROUP_CH), (2 * SSM_GROUP_CH) ** -0.5),
        "ssm_b_im": nrm(ks[11], (L, SSM_GROUPS, SSM_STATE, SSM_GROUP_CH), (2 * SSM_GROUP_CH) ** -0.5),
        "ssm_c_re": nrm(ks[12], (L, SSM_GROUPS, SSM_GROUP_CH, SSM_STATE), SSM_STATE ** -0.5),
        "ssm_c_im": nrm(ks[13], (L, SSM_GROUPS, SSM_GROUP_CH, SSM_STATE), SSM_STATE ** -0.5),
        "ssm_d": nrm(ks[14], (L, SSM_GROUPS, SSM_GROUP_CH), 1.0),
        "w_glu": nrm(ks[15], (L, SSM_WIDTH, SSM_WIDTH), SSM_WIDTH ** -0.5),
        "b_glu": nrm(ks[16], (L, SSM_WIDTH), 0.01),
        "w_out": nrm(ks[17], (L, MIX_WIDTH, D_MODEL), MIX_WIDTH ** -0.5 * beta),
        "ln1_g": 1.0 + nrm(ks[18], (L, D_MODEL), 0.02),
        "ln1_b": nrm(ks[19], (L, D_MODEL), 0.01),
        "w_router_group": nrm(ks[20], (L, D_MODEL, N_EXPERT_GROUPS), D_MODEL ** -0.5),
        "w_router_expert": nrm(ks[21], (L, D_MODEL, N_EXPERTS), D_MODEL ** -0.5),
        "w_exp_gate": nrm(ks[22], (L, N_EXPERTS, D_MODEL, D_EXPERT), D_MODEL ** -0.5),
        "w_exp_up": nrm(ks[23], (L, N_EXPERTS, D_MODEL, D_EXPERT), D_MODEL ** -0.5),
        "w_exp_down": nrm(ks[24], (L, N_EXPERTS, D_EXPERT, D_MODEL), D_EXPERT ** -0.5 * beta),
        "ln2_g": 1.0 + nrm(ks[25], (L, D_MODEL), 0.02),
        "ln2_b": nrm(ks[26], (L, D_MODEL), 0.01),
    }


def reference(x, w_in, lam_q1, lam_k1, lam_q2, lam_k2, subln_g, ssm_a_re, ssm_a_im, ssm_log_dt,
              ssm_b_re, ssm_b_im, ssm_c_re, ssm_c_im, ssm_d, w_glu, b_glu, w_out, ln1_g, ln1_b,
              w_router_group, w_router_expert, w_exp_gate, w_exp_up, w_exp_down, ln2_g, ln2_b):
    alpha = (2.0 * DEPTH) ** 0.25
    b_, s_, d_ = x.shape
    for i in range(DEPTH):
        proj = x @ w_in[i]
        q, k, v, u = jnp.split(proj, [ATTN_WIDTH, 2 * ATTN_WIDTH, 3 * ATTN_WIDTH], axis=-1)
        q = q.reshape(b_, s_, N_DIFF_HEADS, 2, DIFF_HEAD_DIM)
        k = k.reshape(b_, s_, N_DIFF_HEADS, 2, DIFF_HEAD_DIM)
        v = v.reshape(b_, s_, N_DIFF_HEADS, 2 * DIFF_HEAD_DIM)
        lam_init = _lambda_init(i)
        lam = (jnp.exp(jnp.sum(lam_q1[i].astype(jnp.float32) * lam_k1[i].astype(jnp.float32)))
               - jnp.exp(jnp.sum(lam_q2[i].astype(jnp.float32) * lam_k2[i].astype(jnp.float32)))
               + lam_init)
        a_out = _diff_attention(q, k, v, lam, lam_init, subln_g[i])
        s_out = _s5_mixer(u, ssm_a_re[i], ssm_a_im[i], ssm_log_dt[i], ssm_b_re[i], ssm_b_im[i],
                          ssm_c_re[i], ssm_c_im[i], ssm_d[i], w_glu[i], b_glu[i])
        mix = jnp.concatenate([a_out, s_out], axis=-1) @ w_out[i]
        h = _layernorm(alpha * x + mix, ln1_g[i], ln1_b[i])
        moe = _hier_moe(h.reshape(b_ * s_, d_), w_router_group[i], w_router_expert[i],
                        w_exp_gate[i], w_exp_up[i], w_exp_down[i]).reshape(b_, s_, d_)
        x = _layernorm(alpha * h + moe, ln2_g[i], ln2_b[i])
    return x
```

```python
import functools
import math

import jax
import jax.numpy as jnp
from jax import lax
from jax.experimental import pallas as pl
from jax.experimental.pallas import tpu as pltpu

F32 = jnp.float32
BF16 = jnp.bfloat16
I32 = jnp.int32

DEPTH = 1
N_HEADS = 4
HEAD_DIM = 64
ATTN_WIDTH = N_HEADS * 2 * HEAD_DIM
CHUNK = 64
GROUP_CH = 16
SSM_STATE = 64
N_EXPERT_GROUPS = 4
EXPERTS_PER_GROUP = 8
N_EXPERTS = N_EXPERT_GROUPS * EXPERTS_PER_GROUP
LN_EPS = 1e-5
RMS_EPS = 1e-5

LANES = 128
SUBLANES = 8
VMEM_LIMIT = 56 * 1024 * 1024

NEG_BIG = -1e30


def _in_proj_body(x_ref, w_ref, q_ref, k_ref, v_ref, u_ref, *, width, q_scale):
    x = x_ref[0].astype(BF16)

    def proj(c):
        return jnp.dot(x, w_ref[:, c * width:(c + 1) * width], preferred_element_type=F32)

    q_ref[0] = (proj(0) * q_scale).astype(BF16)
    k_ref[0] = proj(1).astype(BF16)
    v_ref[0] = proj(2).astype(BF16)
    u_ref[0] = proj(3)


def _in_proj(x, w_in_bf16, *, ts):
    b, s, d = x.shape
    width = ATTN_WIDTH
    out_block = pl.BlockSpec((1, ts, width), lambda bi, i: (bi, i, 0))
    return pl.pallas_call(
        functools.partial(_in_proj_body, width=width, q_scale=HEAD_DIM ** -0.5),
        grid=(b, s // ts),
        in_specs=[
            pl.BlockSpec((1, ts, d), lambda bi, i: (bi, i, 0)),
            pl.BlockSpec((d, 4 * width), lambda bi, i: (0, 0)),
        ],
        out_specs=[out_block, out_block, out_block, out_block],
        out_shape=[
            jax.ShapeDtypeStruct((b, s, width), BF16),
            jax.ShapeDtypeStruct((b, s, width), BF16),
            jax.ShapeDtypeStruct((b, s, width), BF16),
            jax.ShapeDtypeStruct((b, s, width), F32),
        ],
        compiler_params=pltpu.CompilerParams(
            dimension_semantics=("parallel", "parallel"), vmem_limit_bytes=VMEM_LIMIT),
        name="in_proj",
    )(x, w_in_bf16)


def _attn_body(lamp_ref, g_ref, q_ref, k_ref, v_ref, o_ref, *, lam_init, n_qt, qb):
    lp = lamp_ref[...]
    s1 = jnp.sum(lp[0:1] * lp[1:2], axis=-1, keepdims=True)
    s2 = jnp.sum(lp[2:3] * lp[3:4], axis=-1, keepdims=True)
    lam = jnp.exp(s1) - jnp.exp(s2) + lam_init

    lane = lax.broadcasted_iota(I32, (qb, 2 * HEAD_DIM), 1)
    row = lax.broadcasted_iota(I32, (2 * qb, qb), 0)
    col = lax.broadcasted_iota(I32, (2 * qb, qb), 1)
    row = jnp.where(row >= qb, row - qb, row)
    diag_mask = (col // CHUNK) <= (row // CHUNK)

    def q_tile(i, carry_unused):
        q = q_ref[0, pl.ds(pl.multiple_of(i * qb, qb), qb), :]
        zero = jnp.zeros_like(q)
        qq = jnp.concatenate(
            [jnp.where(lane < HEAD_DIM, q, zero), jnp.where(lane >= HEAD_DIM, q, zero)], axis=0)

        def kv_step(j, carry, masked):
            m, l, acc = carry
            start = pl.multiple_of(j * qb, qb)
            kb = k_ref[0, pl.ds(start, qb), :]
            vb = v_ref[0, pl.ds(start, qb), :]
            s = lax.dot_general(qq, kb, (((1,), (1,)), ((), ())), preferred_element_type=F32)
            if masked:
                s = jnp.where(diag_mask, s, NEG_BIG)
            m_new = jnp.maximum(m, jnp.max(s, axis=-1, keepdims=True))
            p = jnp.exp(s - m_new)
            a = jnp.exp(m - m_new)
            l = a * l + jnp.sum(p, axis=-1, keepdims=True)
            acc = a * acc + jnp.dot(p.astype(BF16), vb, preferred_element_type=F32)
            return m_new, l, acc

        carry = (jnp.full((2 * qb, 1), NEG_BIG, F32), jnp.zeros((2 * qb, 1), F32),
                 jnp.zeros((2 * qb, 2 * HEAD_DIM), F32))
        carry = lax.fori_loop(0, i, lambda j, c: kv_step(j, c, False), carry)
        _, l, acc = kv_step(i, carry, True)
        o = acc / l
        od = o[:qb] - lam * o[qb:]
        ms = jnp.mean(od * od, axis=-1, keepdims=True)
        od = od * lax.rsqrt(ms + RMS_EPS) * g_ref[...] * (1.0 - lam_init)
        o_ref[0, pl.ds(pl.multiple_of(i * qb, qb), qb), :] = od.astype(BF16)
        return carry_unused

    lax.fori_loop(0, n_qt, q_tile, 0)


def _diff_attention(q, k, v, lam_params, subln_g, *, lam_init, qb):
    b, s, width = q.shape
    hw = 2 * HEAD_DIM
    seq_block = pl.BlockSpec((1, s, hw), lambda bi, h: (bi, 0, h))
    return pl.pallas_call(
        functools.partial(_attn_body, lam_init=lam_init, n_qt=s // qb, qb=qb),
        grid=(b, N_HEADS),
        in_specs=[
            pl.BlockSpec((4, HEAD_DIM), lambda bi, h: (0, 0)),
            pl.BlockSpec((1, hw), lambda bi, h: (0, 0)),
            seq_block, seq_block, seq_block,
        ],
        out_specs=seq_block,
        out_shape=jax.ShapeDtypeStruct((b, s, width), BF16),
        compiler_params=pltpu.CompilerParams(
            dimension_semantics=("parallel", "parallel"), vmem_limit_bytes=VMEM_LIMIT),
        name="diff_attn",
    )(lam_params, subln_g, q, k, v)


def _ssm_body(u_ref, bbd_ref, cbd_ref, lre_ref, lim_ref, d_ref, wg_ref, bg_ref, z_ref,
              scr, hst, ysc, *, tb, tbp, n_lt):
    nb = u_ref.shape[0]
    n_half = 4

    @pl.when(pl.program_id(0) == 0)
    def _():
        hst[...] = jnp.zeros_like(hst)

    for lt in range(n_lt):
        lanes = slice(lt * LANES, (lt + 1) * LANES)
        for b in range(nb):
            ub = u_ref[b, :, lanes].astype(BF16)
            bu = jnp.dot(ub, bbd_ref[lt], preferred_element_type=F32)
            for k in range(2 * n_half):
                scr[k, b * tbp:b * tbp + tb, :] = bu[:, k * LANES:(k + 1) * LANES]

        lr = [jnp.broadcast_to(lre_ref[lt, :, k * LANES:(k + 1) * LANES], (nb, LANES))
              for k in range(n_half)]
        li = [jnp.broadcast_to(lim_ref[lt, :, k * LANES:(k + 1) * LANES], (nb, LANES))
              for k in range(n_half)]

        def step(t, h):
            new_re, new_im = [], []
            for k in range(n_half):
                rows = pl.ds(t, nb, stride=tbp)
                hr, hi = h[k], h[n_half + k]
                nr = lr[k] * hr - li[k] * hi + scr[k, rows, :]
                ni = lr[k] * hi + li[k] * hr + scr[n_half + k, rows, :]
                scr[k, rows, :] = nr
                scr[n_half + k, rows, :] = ni
                new_re.append(nr)
                new_im.append(ni)
            return tuple(new_re + new_im)

        h = lax.fori_loop(0, tb, step, tuple(hst[lt, k] for k in range(2 * n_half)))
        for k in range(2 * n_half):
            hst[lt, k] = h[k]

        for b in range(nb):
            hb = jnp.concatenate(
                [scr[k, b * tbp:b * tbp + tb, :] for k in range(2 * n_half)], axis=1)
            yb = jnp.dot(hb.astype(BF16), cbd_ref[lt], preferred_element_type=F32)
            ysc[b, :, lanes] = yb + d_ref[:, lanes] * u_ref[b, :, lanes]

    for b in range(nb):
        z = jax.nn.gelu(ysc[b])
        gate = jax.nn.sigmoid(
            jnp.dot(z.astype(BF16), wg_ref[...], preferred_element_type=F32) + bg_ref[...])
        z_ref[b] = (z * gate).astype(BF16)


def _ssm_params(a_re, a_im, log_dt, b_re, b_im, c_re, c_im):
    g = a_re.shape[0]
    n_lt = g * GROUP_CH // LANES
    gpt = g // n_lt
    dt = jnp.exp(log_dt.astype(F32))[:, None]
    ar, ai = a_re.astype(F32), a_im.astype(F32)
    mag = jnp.exp(ar * dt)
    lam_re = mag * jnp.cos(ai * dt)
    lam_im = mag * jnp.sin(ai * dt)
    nr, ni = lam_re - 1.0, lam_im
    den = ar * ar + ai * ai
    cr = (nr * ar + ni * ai) / den
    ci = (ni * ar - nr * ai) / den
    bb_re = cr[..., None] * b_re - ci[..., None] * b_im
    bb_im = cr[..., None] * b_im + ci[..., None] * b_re
    eye = jnp.eye(gpt, dtype=F32)

    def pack_b(bb):
        bb = bb.reshape(n_lt, gpt, SSM_STATE, GROUP_CH)
        m = jnp.einsum('lgpc,gh->lgchp', bb, eye)
        return m.reshape(n_lt, gpt * GROUP_CH, gpt * SSM_STATE)

    def pack_c(cc):
        cc = cc.astype(F32).reshape(n_lt, gpt, GROUP_CH, SSM_STATE)
        m = jnp.einsum('lgcp,gh->lgphc', cc, eye)
        return m.reshape(n_lt, gpt * SSM_STATE, gpt * GROUP_CH)

    bbd = jnp.concatenate([pack_b(bb_re), pack_b(bb_im)], axis=2).astype(BF16)
    cbd = jnp.concatenate([pack_c(c_re), -pack_c(c_im)], axis=1).astype(BF16)
    lre = lam_re.reshape(n_lt, 1, gpt * SSM_STATE)
    lim = lam_im.reshape(n_lt, 1, gpt * SSM_STATE)
    return bbd, cbd, lre, lim


def _ssm_glu(u, bbd, cbd, lre, lim, d, w_glu_bf16, b_glu, *, tb):
    b, s, width = u.shape
    assert b == SUBLANES
    n_lt = width // LANES
    tbp = tb + SUBLANES
    full3 = lambda a: pl.BlockSpec(a.shape, lambda i: (0, 0, 0))
    full2 = lambda a: pl.BlockSpec(a.shape, lambda i: (0, 0))
    return pl.pallas_call(
        functools.partial(_ssm_body, tb=tb, tbp=tbp, n_lt=n_lt),
        grid=(s // tb,),
        in_specs=[
            pl.BlockSpec((b, tb, width), lambda i: (0, i, 0)),
            full3(bbd), full3(cbd), full3(lre), full3(lim), full2(d), full2(w_glu_bf16),
            full2(b_glu),
        ],
        out_specs=pl.BlockSpec((b, tb, width), lambda i: (0, i, 0)),
        out_shape=jax.ShapeDtypeStruct((b, s, width), BF16),
        scratch_shapes=[
            pltpu.VMEM((8, b * tbp, LANES), F32),
            pltpu.VMEM((n_lt, 8, b, LANES), F32),
            pltpu.VMEM((b, tb, width), F32),
        ],
        compiler_params=pltpu.CompilerParams(
            dimension_semantics=("arbitrary",), vmem_limit_bytes=VMEM_LIMIT),
        name="ssm_glu",
    )(u, bbd, cbd, lre, lim, d, w_glu_bf16, b_glu)


def _layernorm(y, g, b):
    mu = jnp.mean(y, axis=-1, keepdims=True)
    yc = y - mu
    var = jnp.mean(yc * yc, axis=-1, keepdims=True)
    return yc * lax.rsqrt(var + LN_EPS) * g + b


N_ROUTE_COLS = N_EXPERT_GROUPS + N_EXPERTS


def _out_proj_body(a_ref, z_ref, x_ref, wo_ref, g_ref, b_ref, wr_ref, h_ref, route_ref, *, alpha):
    aw = a_ref.shape[2]
    mix = jnp.dot(a_ref[0], wo_ref[0:aw, :], preferred_element_type=F32)
    mix += jnp.dot(z_ref[0], wo_ref[aw:, :], preferred_element_type=F32)
    h = _layernorm(alpha * x_ref[0] + mix, g_ref[...], b_ref[...])
    h_ref[0] = h

    h_hi = h.astype(BF16)
    h_lo = (h - h_hi.astype(F32)).astype(BF16)
    lg = jnp.dot(h_hi, wr_ref[...], preferred_element_type=F32)
    lg += jnp.dot(h_lo, wr_ref[...], preferred_element_type=F32)
    lg = lg + pltpu.roll(lg, LANES - N_ROUTE_COLS, 1)

    lane = lax.broadcasted_iota(I32, lg.shape, 1)

    def argmax_lane(vals):
        mx = jnp.max(vals, axis=-1, keepdims=True)
        idx = jnp.min(jnp.where(vals == mx, lane, LANES), axis=-1, keepdims=True)
        return mx, idx

    gl = jnp.where(lane < N_EXPERT_GROUPS, lg, NEG_BIG)
    g_max, g_idx = argmax_lane(gl)
    g_p = 1.0 / jnp.sum(jnp.exp(gl - g_max), axis=-1, keepdims=True)

    e_lo = N_EXPERT_GROUPS + g_idx * EXPERTS_PER_GROUP
    el = jnp.where((lane >= e_lo) & (lane < e_lo + EXPERTS_PER_GROUP), lg, NEG_BIG)
    e1, i1 = argmax_lane(el)
    e2, i2 = argmax_lane(jnp.where(lane == i1, NEG_BIG, el))
    r = jnp.exp(e2 - e1)
    w1 = g_p / (1.0 + r)
    w2 = g_p * r / (1.0 + r)
    route = jnp.where(lane == 0, (i1 - N_EXPERT_GROUPS).astype(F32), 0.0)
    route = jnp.where(lane == 1, (i2 - N_EXPERT_GROUPS).astype(F32), route)
    route = jnp.where(lane == 2, w1, route)
    route = jnp.where(lane == 3, w2, route)
    route_ref[0] = route


def _out_proj(a, z, x, w_out_bf16, ln_g, ln_b, w_route, *, alpha, ts):
    b, s, d = x.shape
    aw = a.shape[2]
    half = pl.BlockSpec((1, ts, aw), lambda bi, i: (bi, i, 0))
    row = pl.BlockSpec((1, ts, d), lambda bi, i: (bi, i, 0))
    full2 = lambda arr: pl.BlockSpec(arr.shape, lambda bi, i: (0, 0))
    return pl.pallas_call(
        functools.partial(_out_proj_body, alpha=alpha),
        grid=(b, s // ts),
        in_specs=[half, half, row, full2(w_out_bf16), full2(ln_g), full2(ln_b), full2(w_route)],
        out_specs=[row, pl.BlockSpec((1, ts, LANES), lambda bi, i: (bi, i, 0))],
        out_shape=[
            jax.ShapeDtypeStruct((b, s, d), F32),
            jax.ShapeDtypeStruct((b, s, LANES), F32),
        ],
        compiler_params=pltpu.CompilerParams(
            dimension_semantics=("parallel", "parallel"), vmem_limit_bytes=VMEM_LIMIT),
        name="out_proj_router",
    )(a, z, x, w_out_bf16, ln_g, ln_b, w_route)


def _moe_body(be_ref, nused_ref, tok_ref, rw_ref, h_hbm, wg_ref, wu_ref, wd_ref, ys_ref,
              xbuf, wgb, wub, wdb, sem, *, bm):
    i = pl.program_id(0)
    used = i < nused_ref[0]

    @pl.when(used)
    def _():
        def issue(r, c):
            tok = tok_ref[0, 0, r]
            pltpu.make_async_copy(h_hbm.at[pl.ds(tok, 1)], xbuf.at[pl.ds(r, 1)], sem).start()
            return c

        lax.fori_loop(0, bm, issue, 0)

        prev = be_ref[jnp.maximum(i - 1, 0)]

        @pl.when((i == 0) | (be_ref[i] != prev))
        def _():
            wgb[...] = wg_ref[0].astype(BF16)
            wub[...] = wu_ref[0].astype(BF16)
            wdb[...] = wd_ref[0].astype(BF16)

        pltpu.make_async_copy(h_hbm.at[pl.ds(0, bm)], xbuf, sem).wait()

        x = xbuf[...].astype(BF16)
        gate = jnp.dot(x, wgb[...], preferred_element_type=F32)
        up = jnp.dot(x, wub[...], preferred_element_type=F32)
        hid = (jax.nn.silu(gate) * up).astype(BF16)
        y = jnp.dot(hid, wdb[...], preferred_element_type=F32)
        ys_ref[...] = y * rw_ref[...]

    @pl.when(jnp.logical_not(used))
    def _():
        ys_ref[...] = jnp.zeros_like(ys_ref)


def _moe_ffn(block_e, n_used, row_tok, row_w, h2d, w_gate, w_up, w_down, *, bm):
    t, d = h2d.shape
    n_rows = row_tok.shape[0]
    n_blocks = n_rows // bm
    de = w_gate.shape[2]
    grid_spec = pltpu.PrefetchScalarGridSpec(
        num_scalar_prefetch=2,
        grid=(n_blocks,),
        in_specs=[
            pl.BlockSpec((1, 1, bm), lambda i, be, nu: (i, 0, 0), memory_space=pltpu.SMEM),
            pl.BlockSpec((bm, 1), lambda i, be, nu: (i, 0)),
            pl.BlockSpec(memory_space=pl.ANY),
            pl.BlockSpec((1, d, de), lambda i, be, nu: (be[i], 0, 0)),
            pl.BlockSpec((1, d, de), lambda i, be, nu: (be[i], 0, 0)),
            pl.BlockSpec((1, de, d), lambda i, be, nu: (be[i], 0, 0)),
        ],
        out_specs=pl.BlockSpec((bm, d), lambda i, be, nu: (i, 0)),
        scratch_shapes=[
            pltpu.VMEM((bm, d), F32),
            pltpu.VMEM((d, de), BF16),
            pltpu.VMEM((d, de), BF16),
            pltpu.VMEM((de, d), BF16),
            pltpu.SemaphoreType.DMA(()),
        ],
    )
    return pl.pallas_call(
        functools.partial(_moe_body, bm=bm),
        grid_spec=grid_spec,
        out_shape=jax.ShapeDtypeStruct((n_rows, d), F32),
        compiler_params=pltpu.CompilerParams(
            dimension_semantics=("arbitrary",), vmem_limit_bytes=VMEM_LIMIT),
        name="moe_ffn",
    )(block_e, n_used, row_tok.reshape(n_blocks, 1, bm), row_w.reshape(n_rows, 1), h2d,
      w_gate, w_up, w_down)


def _combine_body(pos_ref, ys_hbm, h_ref, g_ref, b_ref, o_ref, gbuf, sem, *, tt, alpha):
    def issue(r, c):
        p = pos_ref[0, 0, r]
        pltpu.make_async_copy(ys_hbm.at[pl.ds(p, 1)], gbuf.at[pl.ds(r, 1)], sem).start()
        return c

    lax.fori_loop(0, 2 * tt, issue, 0)
    pltpu.make_async_copy(ys_hbm.at[pl.ds(0, 2 * tt)], gbuf, sem).wait()
    moe = gbuf[0:tt, :] + gbuf[tt:2 * tt, :]
    o_ref[...] = _layernorm(alpha * h_ref[...] + moe, g_ref[...], b_ref[...])


def _combine(pos, ys, h2d, ln_g, ln_b, *, alpha, tt):
    t, d = h2d.shape
    n_tiles = t // tt
    return pl.pallas_call(
        functools.partial(_combine_body, tt=tt, alpha=alpha),
        grid=(n_tiles,),
        in_specs=[
            pl.BlockSpec((1, 1, 2 * tt), lambda i: (i, 0, 0), memory_space=pltpu.SMEM),
            pl.BlockSpec(memory_space=pl.ANY),
            pl.BlockSpec((tt, d), lambda i: (i, 0)),
            pl.BlockSpec((1, d), lambda i: (0, 0)),
            pl.BlockSpec((1, d), lambda i: (0, 0)),
        ],
        out_specs=pl.BlockSpec((tt, d), lambda i: (i, 0)),
        out_shape=jax.ShapeDtypeStruct((t, d), F32),
        scratch_shapes=[pltpu.VMEM((2 * tt, d), F32), pltpu.SemaphoreType.DMA(())],
        compiler_params=pltpu.CompilerParams(
            dimension_semantics=("arbitrary",), vmem_limit_bytes=VMEM_LIMIT),
        name="moe_combine",
    )(pos, ys, h2d, ln_g, ln_b)


def _moe_layout(route, *, bm, tt):
    t = route.shape[0]
    tk = 2 * t
    flat_e = route[:, 0:2].astype(I32).reshape(-1)
    flat_w = route[:, 2:4].reshape(-1)
    se, order = lax.sort((flat_e, jnp.arange(tk, dtype=I32)), num_keys=1, is_stable=True)
    counts = jnp.sum((flat_e[:, None] == jnp.arange(N_EXPERTS, dtype=I32)[None, :]).astype(I32), axis=0)
    starts = jnp.cumsum(counts) - counts
    pcounts = (counts + bm - 1) // bm * bm
    pends = jnp.cumsum(pcounts)
    pstarts = pends - pcounts
    dest_sorted = pstarts[se] + jnp.arange(tk, dtype=I32) - starts[se]
    n_rows = tk + N_EXPERTS * bm
    n_blocks = n_rows // bm
    row_tok = jnp.zeros((n_rows,), I32).at[dest_sorted].set(
        order // 2, indices_are_sorted=True, unique_indices=True)
    row_w = jnp.zeros((n_rows,), F32).at[dest_sorted].set(
        flat_w[order], indices_are_sorted=True, unique_indices=True)
    dest = jnp.zeros((tk,), I32).at[order].set(dest_sorted, unique_indices=True)
    block_e = jnp.minimum(
        jnp.searchsorted(pends, jnp.arange(n_blocks, dtype=I32) * bm, side='right'),
        N_EXPERTS - 1).astype(I32)
    n_used = (pends[-1] // bm).astype(I32).reshape(1)
    pos = dest.reshape(t // tt, tt, 2).transpose(0, 2, 1).reshape(t // tt, 1, 2 * tt)
    return block_e, n_used, row_tok, row_w, pos


def _layer(x, w_in, lam_q1, lam_k1, lam_q2, lam_k2, subln_g, a_re, a_im, log_dt, b_re, b_im,
           c_re, c_im, d, w_glu, b_glu, w_out, ln1_g, ln1_b, w_rg, w_re, w_gate, w_up, w_down,
           ln2_g, ln2_b, *, layer_idx, alpha, ts, qb, tb, bm, tt):
    b, s, dm = x.shape
    lam_init = 0.8 - 0.6 * math.exp(-0.3 * layer_idx)

    q, k, v, u = _in_proj(x, w_in.astype(BF16), ts=ts)

    lam_params = jnp.stack([lam_q1, lam_k1, lam_q2, lam_k2]).astype(F32)
    a_out = _diff_attention(q, k, v, lam_params, subln_g.reshape(1, -1).astype(F32),
                            lam_init=lam_init, qb=qb)

    bbd, cbd, lre, lim = _ssm_params(a_re, a_im, log_dt, b_re, b_im, c_re, c_im)
    s_out = _ssm_glu(u, bbd, cbd, lre, lim, d.reshape(1, -1).astype(F32), w_glu.astype(BF16),
                     b_glu.reshape(1, -1).astype(F32), tb=tb)

    w_r = jnp.concatenate([w_rg, w_re], axis=1).astype(F32)
    w_r_hi = w_r.astype(BF16)
    w_r_lo = (w_r - w_r_hi.astype(F32)).astype(BF16)
    w_route = jnp.concatenate(
        [w_r_hi, w_r_lo, jnp.zeros((dm, LANES - 2 * N_ROUTE_COLS), BF16)], axis=1)
    h, route = _out_proj(a_out, s_out, x, w_out.astype(BF16), ln1_g.reshape(1, -1),
                         ln1_b.reshape(1, -1), w_route, alpha=alpha, ts=ts)

    h2d = h.reshape(b * s, dm)
    block_e, n_used, row_tok, row_w, pos = _moe_layout(route.reshape(b * s, LANES), bm=bm, tt=tt)
    ys = _moe_ffn(block_e, n_used, row_tok, row_w, h2d, w_gate, w_up, w_down, bm=bm)
    out = _combine(pos, ys, h2d, ln2_g.reshape(1, -1), ln2_b.reshape(1, -1), alpha=alpha, tt=tt)
    return out.reshape(b, s, dm)


def kernel(x, w_in, lam_q1, lam_k1, lam_q2, lam_k2, subln_g, ssm_a_re, ssm_a_im, ssm_log_dt, ssm_b_re, ssm_b_im, ssm_c_re, ssm_c_im, ssm_d, w_glu, b_glu, w_out, ln1_g, ln1_b, w_router_group, w_router_expert, w_exp_gate, w_exp_up, w_exp_down, ln2_g, ln2_b):
    depth = w_in.shape[0]
    alpha = (2.0 * depth) ** 0.25
    for i in range(depth):
        x = _layer(
            x, w_in[i], lam_q1[i], lam_k1[i], lam_q2[i], lam_k2[i], subln_g[i], ssm_a_re[i],
            ssm_a_im[i], ssm_log_dt[i], ssm_b_re[i], ssm_b_im[i], ssm_c_re[i], ssm_c_im[i],
            ssm_d[i], w_glu[i], b_glu[i], w_out[i], ln1_g[i], ln1_b[i], w_router_group[i],
            w_router_expert[i], w_exp_gate[i], w_exp_up[i], w_exp_down[i], ln2_g[i], ln2_b[i],
            layer_idx=i, alpha=alpha, ts=512, qb=256, tb=256, bm=256, tt=256)
    return x
```

```python
import functools
import math

import jax
import jax.numpy as jnp
from jax import lax
from jax.experimental import pallas as pl
from jax.experimental.pallas import tpu as pltpu

F32 = jnp.float32
BF16 = jnp.bfloat16
I32 = jnp.int32

DEPTH = 1
N_HEADS = 4
HEAD_DIM = 64
ATTN_WIDTH = N_HEADS * 2 * HEAD_DIM
CHUNK = 64
GROUP_CH = 16
SSM_STATE = 64
N_EXPERT_GROUPS = 4
EXPERTS_PER_GROUP = 8
N_EXPERTS = N_EXPERT_GROUPS * EXPERTS_PER_GROUP
LN_EPS = 1e-5
RMS_EPS = 1e-5

LANES = 128
SUBLANES = 8
VMEM_LIMIT = 56 * 1024 * 1024

NEG_BIG = -1e30


def _in_proj_body(x_ref, w_ref, q_ref, k_ref, v_ref, u_ref, *, width, q_scale):
    x = x_ref[0].astype(BF16)

    def proj(c):
        return jnp.dot(x, w_ref[:, c * width:(c + 1) * width], preferred_element_type=F32)

    q_ref[0] = (proj(0) * q_scale).astype(BF16)
    k_ref[0] = proj(1).astype(BF16)
    v_ref[0] = proj(2).astype(BF16)
    u_ref[0] = proj(3)


def _in_proj(x, w_in_bf16, *, ts):
    b, s, d = x.shape
    width = ATTN_WIDTH
    out_block = pl.BlockSpec((1, ts, width), lambda bi, i: (bi, i, 0))
    return pl.pallas_call(
        functools.partial(_in_proj_body, width=width, q_scale=HEAD_DIM ** -0.5 * math.log2(math.e)),
        grid=(b, s // ts),
        in_specs=[
            pl.BlockSpec((1, ts, d), lambda bi, i: (bi, i, 0)),
            pl.BlockSpec((d, 4 * width), lambda bi, i: (0, 0)),
        ],
        out_specs=[out_block, out_block, out_block, out_block],
        out_shape=[
            jax.ShapeDtypeStruct((b, s, width), BF16),
            jax.ShapeDtypeStruct((b, s, width), BF16),
            jax.ShapeDtypeStruct((b, s, width), BF16),
            jax.ShapeDtypeStruct((b, s, width), F32),
        ],
        compiler_params=pltpu.CompilerParams(
            dimension_semantics=("parallel", "parallel"), vmem_limit_bytes=VMEM_LIMIT),
        name="in_proj",
    )(x, w_in_bf16)


def _attn_body(lamp_ref, g_ref, q_ref, k_ref, v_ref, o_ref, *, lam_init, n_qt, qb, kb):
    lp = lamp_ref[...]
    s1 = jnp.sum(lp[0:1] * lp[1:2], axis=-1, keepdims=True)
    s2 = jnp.sum(lp[2:3] * lp[3:4], axis=-1, keepdims=True)
    lam = jnp.exp(s1) - jnp.exp(s2) + lam_init

    lane = lax.broadcasted_iota(I32, (qb, 2 * HEAD_DIM), 1)
    row = lax.broadcasted_iota(I32, (2 * qb, kb), 0)
    col = lax.broadcasted_iota(I32, (2 * qb, kb), 1)
    row = jnp.where(row >= qb, row - qb, row)
    chunk_gap = col // CHUNK - row // CHUNK

    def q_tile(i, carry_unused):
        q = q_ref[0, pl.ds(pl.multiple_of(i * qb, qb), qb), :]
        zero = jnp.zeros_like(q)
        qq = jnp.concatenate(
            [jnp.where(lane < HEAD_DIM, q, zero), jnp.where(lane >= HEAD_DIM, q, zero)], axis=0)

        def kv_step(j, carry, masked):
            m, l, acc = carry
            start = pl.multiple_of(j * kb, kb)
            kblk = k_ref[0, pl.ds(start, kb), :]
            vblk = v_ref[0, pl.ds(start, kb), :]
            s = lax.dot_general(qq, kblk, (((1,), (1,)), ((), ())), preferred_element_type=F32)
            if masked:
                allowed = chunk_gap <= (i * qb - j * kb) // CHUNK
                s = jnp.where(allowed, s, NEG_BIG)
            m_new = jnp.maximum(m, jnp.max(s, axis=-1, keepdims=True))
            p = jnp.exp2(s - m_new)
            a = jnp.exp2(m - m_new)
            l = a * l + jnp.sum(p, axis=-1, keepdims=True)
            acc = a * acc + jnp.dot(p.astype(BF16), vblk, preferred_element_type=F32)
            return m_new, l, acc

        carry = (jnp.full((2 * qb, 1), NEG_BIG, F32), jnp.zeros((2 * qb, 1), F32),
                 jnp.zeros((2 * qb, 2 * HEAD_DIM), F32))
        n_full = (i * qb) // kb
        carry = lax.fori_loop(0, n_full, lambda j, c: kv_step(j, c, False), carry)
        _, l, acc = kv_step(n_full, carry, True)
        o = acc / l
        od = o[:qb] - lam * o[qb:]
        ms = jnp.mean(od * od, axis=-1, keepdims=True)
        od = od * lax.rsqrt(ms + RMS_EPS) * g_ref[...] * (1.0 - lam_init)
        o_ref[0, pl.ds(pl.multiple_of(i * qb, qb), qb), :] = od.astype(BF16)
        return carry_unused

    lax.fori_loop(0, n_qt, q_tile, 0)


def _diff_attention(q, k, v, lam_params, subln_g, *, lam_init, qb):
    b, s, width = q.shape
    hw = 2 * HEAD_DIM
    kb = min(2 * qb, s)
    seq_block = pl.BlockSpec((1, s, hw), lambda bi, h: (bi, 0, h))
    return pl.pallas_call(
        functools.partial(_attn_body, lam_init=lam_init, n_qt=s // qb, qb=qb, kb=kb),
        grid=(b, N_HEADS),
        in_specs=[
            pl.BlockSpec((4, HEAD_DIM), lambda bi, h: (0, 0)),
            pl.BlockSpec((1, hw), lambda bi, h: (0, 0)),
            seq_block, seq_block, seq_block,
        ],
        out_specs=seq_block,
        out_shape=jax.ShapeDtypeStruct((b, s, width), BF16),
        compiler_params=pltpu.CompilerParams(
            dimension_semantics=("parallel", "parallel"), vmem_limit_bytes=VMEM_LIMIT),
        name="diff_attn",
    )(lam_params, subln_g, q, k, v)


def _ssm_body(u_ref, bbd_ref, cbd_ref, lre_ref, lim_ref, d_ref, wg_ref, bg_ref, z_ref,
              scr, hst, ysc, *, tb, tbp, n_lt):
    nb = u_ref.shape[0]
    n_half = 4

    @pl.when(pl.program_id(0) == 0)
    def _():
        hst[...] = jnp.zeros_like(hst)

    for lt in range(n_lt):
        lanes = slice(lt * LANES, (lt + 1) * LANES)
        for b in range(nb):
            ub = u_ref[b, :, lanes].astype(BF16)
            bu = jnp.dot(ub, bbd_ref[lt], preferred_element_type=F32)
            for k in range(2 * n_half):
                scr[k, b * tbp:b * tbp + tb, :] = bu[:, k * LANES:(k + 1) * LANES]

        lr = [jnp.broadcast_to(lre_ref[lt, :, k * LANES:(k + 1) * LANES], (nb, LANES))
              for k in range(n_half)]
        li = [jnp.broadcast_to(lim_ref[lt, :, k * LANES:(k + 1) * LANES], (nb, LANES))
              for k in range(n_half)]

        def step(t, h):
            new_re, new_im = [], []
            for k in range(n_half):
                rows = pl.ds(t, nb, stride=tbp)
                hr, hi = h[k], h[n_half + k]
                nr = lr[k] * hr - li[k] * hi + scr[k, rows, :]
                ni = lr[k] * hi + li[k] * hr + scr[n_half + k, rows, :]
                scr[k, rows, :] = nr
                scr[n_half + k, rows, :] = ni
                new_re.append(nr)
                new_im.append(ni)
            return tuple(new_re + new_im)

        h = lax.fori_loop(0, tb, step, tuple(hst[lt, k] for k in range(2 * n_half)))
        for k in range(2 * n_half):
            hst[lt, k] = h[k]

        for b in range(nb):
            hb = jnp.concatenate(
                [scr[k, b * tbp:b * tbp + tb, :] for k in range(2 * n_half)], axis=1)
            yb = jnp.dot(hb.astype(BF16), cbd_ref[lt], preferred_element_type=F32)
            ysc[b, :, lanes] = yb + d_ref[:, lanes] * u_ref[b, :, lanes]

    for b in range(nb):
        z = jax.nn.gelu(ysc[b])
        gate = jax.nn.sigmoid(
            jnp.dot(z.astype(BF16), wg_ref[...], preferred_element_type=F32) + bg_ref[...])
        z_ref[b] = (z * gate).astype(BF16)


def _ssm_params(a_re, a_im, log_dt, b_re, b_im, c_re, c_im):
    g = a_re.shape[0]
    n_lt = g * GROUP_CH // LANES
    gpt = g // n_lt
    dt = jnp.exp(log_dt.astype(F32))[:, None]
    ar, ai = a_re.astype(F32), a_im.astype(F32)
    mag = jnp.exp(ar * dt)
    lam_re = mag * jnp.cos(ai * dt)
    lam_im = mag * jnp.sin(ai * dt)
    nr, ni = lam_re - 1.0, lam_im
    den = ar * ar + ai * ai
    cr = (nr * ar + ni * ai) / den
    ci = (ni * ar - nr * ai) / den
    bb_re = cr[..., None] * b_re - ci[..., None] * b_im
    bb_im = cr[..., None] * b_im + ci[..., None] * b_re
    eye = jnp.eye(gpt, dtype=F32)

    def pack_b(bb):
        bb = bb.reshape(n_lt, gpt, SSM_STATE, GROUP_CH)
        m = jnp.einsum('lgpc,gh->lgchp', bb, eye)
        return m.reshape(n_lt, gpt * GROUP_CH, gpt * SSM_STATE)

    def pack_c(cc):
        cc = cc.astype(F32).reshape(n_lt, gpt, GROUP_CH, SSM_STATE)
        m = jnp.einsum('lgcp,gh->lgphc', cc, eye)
        return m.reshape(n_lt, gpt * SSM_STATE, gpt * GROUP_CH)

    bbd = jnp.concatenate([pack_b(bb_re), pack_b(bb_im)], axis=2).astype(BF16)
    cbd = jnp.concatenate([pack_c(c_re), -pack_c(c_im)], axis=1).astype(BF16)
    lre = lam_re.reshape(n_lt, 1, gpt * SSM_STATE)
    lim = lam_im.reshape(n_lt, 1, gpt * SSM_STATE)
    return bbd, cbd, lre, lim


def _ssm_glu(u, bbd, cbd, lre, lim, d, w_glu_bf16, b_glu, *, tb):
    b, s, width = u.shape
    assert b == SUBLANES
    n_lt = width // LANES
    tbp = tb + SUBLANES
    full3 = lambda a: pl.BlockSpec(a.shape, lambda i: (0, 0, 0))
    full2 = lambda a: pl.BlockSpec(a.shape, lambda i: (0, 0))
    return pl.pallas_call(
        functools.partial(_ssm_body, tb=tb, tbp=tbp, n_lt=n_lt),
        grid=(s // tb,),
        in_specs=[
            pl.BlockSpec((b, tb, width), lambda i: (0, i, 0)),
            full3(bbd), full3(cbd), full3(lre), full3(lim), full2(d), full2(w_glu_bf16),
            full2(b_glu),
        ],
        out_specs=pl.BlockSpec((b, tb, width), lambda i: (0, i, 0)),
        out_shape=jax.ShapeDtypeStruct((b, s, width), BF16),
        scratch_shapes=[
            pltpu.VMEM((8, b * tbp, LANES), F32),
            pltpu.VMEM((n_lt, 8, b, LANES), F32),
            pltpu.VMEM((b, tb, width), F32),
        ],
        compiler_params=pltpu.CompilerParams(
            dimension_semantics=("arbitrary",), vmem_limit_bytes=VMEM_LIMIT),
        name="ssm_glu",
    )(u, bbd, cbd, lre, lim, d, w_glu_bf16, b_glu)


def _layernorm(y, g, b):
    mu = jnp.mean(y, axis=-1, keepdims=True)
    yc = y - mu
    var = jnp.mean(yc * yc, axis=-1, keepdims=True)
    return yc * lax.rsqrt(var + LN_EPS) * g + b


N_ROUTE_COLS = N_EXPERT_GROUPS + N_EXPERTS


def _store_token_tiles(ref, val):
    n = val.shape[0]
    for j in range(SUBLANES):
        ref[pl.ds(j, n, stride=SUBLANES), :] = val[:, j * LANES:(j + 1) * LANES]


def _load_token_tiles(ref, n):
    return jnp.concatenate(
        [ref[pl.ds(j, n, stride=SUBLANES), :] for j in range(SUBLANES)], axis=1)


def _out_proj_body(a_ref, z_ref, x_ref, wo_ref, g_ref, b_ref, wr_ref, h8_ref, route_ref, *, alpha):
    aw = a_ref.shape[2]
    mix = jnp.dot(a_ref[0], wo_ref[0:aw, :], preferred_element_type=F32)
    mix += jnp.dot(z_ref[0], wo_ref[aw:, :], preferred_element_type=F32)
    h = _layernorm(alpha * x_ref[0] + mix, g_ref[...], b_ref[...])
    _store_token_tiles(h8_ref, h)

    h_hi = h.astype(BF16)
    h_lo = (h - h_hi.astype(F32)).astype(BF16)
    lg = jnp.dot(h_hi, wr_ref[...], preferred_element_type=F32)
    lg += jnp.dot(h_lo, wr_ref[...], preferred_element_type=F32)
    lg = lg + pltpu.roll(lg, LANES - N_ROUTE_COLS, 1)

    lane = lax.broadcasted_iota(I32, lg.shape, 1)

    def argmax_lane(vals):
        mx = jnp.max(vals, axis=-1, keepdims=True)
        idx = jnp.min(jnp.where(vals == mx, lane, LANES), axis=-1, keepdims=True)
        return mx, idx

    gl = jnp.where(lane < N_EXPERT_GROUPS, lg, NEG_BIG)
    g_max, g_idx = argmax_lane(gl)
    g_p = 1.0 / jnp.sum(jnp.exp(gl - g_max), axis=-1, keepdims=True)

    e_lo = N_EXPERT_GROUPS + g_idx * EXPERTS_PER_GROUP
    el = jnp.where((lane >= e_lo) & (lane < e_lo + EXPERTS_PER_GROUP), lg, NEG_BIG)
    e1, i1 = argmax_lane(el)
    e2, i2 = argmax_lane(jnp.where(lane == i1, NEG_BIG, el))
    r = jnp.exp(e2 - e1)
    w1 = g_p / (1.0 + r)
    w2 = g_p * r / (1.0 + r)
    route = jnp.where(lane == 0, (i1 - N_EXPERT_GROUPS).astype(F32), 0.0)
    route = jnp.where(lane == 1, (i2 - N_EXPERT_GROUPS).astype(F32), route)
    route = jnp.where(lane == 2, w1, route)
    route = jnp.where(lane == 3, w2, route)
    route_ref[0] = route


def _out_proj(a, z, x, w_out_bf16, ln_g, ln_b, w_route, *, alpha, ts):
    b, s, d = x.shape
    aw = a.shape[2]
    half = pl.BlockSpec((1, ts, aw), lambda bi, i: (bi, i, 0))
    row = pl.BlockSpec((1, ts, d), lambda bi, i: (bi, i, 0))
    full2 = lambda arr: pl.BlockSpec(arr.shape, lambda bi, i: (0, 0))
    assert d == SUBLANES * LANES
    n_st = s // ts
    return pl.pallas_call(
        functools.partial(_out_proj_body, alpha=alpha),
        grid=(b, n_st),
        in_specs=[half, half, row, full2(w_out_bf16), full2(ln_g), full2(ln_b), full2(w_route)],
        out_specs=[
            pl.BlockSpec((ts * SUBLANES, LANES), lambda bi, i: (bi * n_st + i, 0)),
            pl.BlockSpec((1, ts, LANES), lambda bi, i: (bi, i, 0)),
        ],
        out_shape=[
            jax.ShapeDtypeStruct((b * s * SUBLANES, LANES), F32),
            jax.ShapeDtypeStruct((b, s, LANES), F32),
        ],
        compiler_params=pltpu.CompilerParams(
            dimension_semantics=("parallel", "parallel"), vmem_limit_bytes=VMEM_LIMIT),
        name="out_proj_router",
    )(a, z, x, w_out_bf16, ln_g, ln_b, w_route)


ISSUE_UNROLL = 8


def _moe_body(be_ref, src_ref, nv_ref, nused_ref, order_ref, h8_hbm, wg_ref, wu_ref, wd_ref,
              out_hbm, xbuf, ybuf, wgb, wub, wdb, gsem, ssem, *, bm, n_tok):
    i = pl.program_id(0)
    n_used = nused_ref[0]
    n_pairs = 2 * n_tok
    slot = i % 2
    tile = SUBLANES

    def gather_copy(blk, buf_slot, r):
        p = order_ref[jnp.minimum(src_ref[blk] + r, n_pairs - 1)]
        tok = p >> 1
        return pltpu.make_async_copy(
            h8_hbm.at[pl.ds(pl.multiple_of(tok * tile, tile), tile)],
            xbuf.at[buf_slot, pl.ds(pl.multiple_of(r * tile, tile), tile)],
            gsem.at[buf_slot])

    def scatter_copy(blk, buf_slot, r):
        p = order_ref[jnp.minimum(src_ref[blk] + r, n_pairs - 1)]
        real = (p & 1) * n_tok + (p >> 1)
        pad = n_pairs + buf_slot * bm + r
        dst = jnp.where(r < nv_ref[blk], real, pad)
        return pltpu.make_async_copy(
            ybuf.at[buf_slot, pl.ds(pl.multiple_of(r * tile, tile), tile)],
            out_hbm.at[pl.ds(pl.multiple_of(dst * tile, tile), tile)],
            ssem.at[buf_slot])

    def issue_all(make_copy, blk, buf_slot):
        def trip(g, c):
            for q in range(ISSUE_UNROLL):
                make_copy(blk, buf_slot, g * ISSUE_UNROLL + q).start()
            return c
        lax.fori_loop(0, bm // ISSUE_UNROLL, trip, 0)

    def wait_gather(buf_slot):
        pltpu.make_async_copy(h8_hbm.at[pl.ds(0, bm * tile)], xbuf.at[buf_slot],
                              gsem.at[buf_slot]).wait()

    def wait_scatter(buf_slot):
        pltpu.make_async_copy(ybuf.at[buf_slot], out_hbm.at[pl.ds(0, bm * tile)],
                              ssem.at[buf_slot]).wait()

    @pl.when(i == 0)
    def _():
        issue_all(gather_copy, 0, 0)
        ybuf[...] = jnp.zeros_like(ybuf)
        for s in range(2):
            pad_rows = out_hbm.at[pl.ds((n_pairs + s * bm) * tile, bm * tile)]
            pltpu.make_async_copy(ybuf.at[s], pad_rows, ssem.at[s]).start()
            pltpu.make_async_copy(ybuf.at[s], pad_rows, ssem.at[s]).wait()

    @pl.when(i < n_used)
    def _():
        wait_gather(slot)

        @pl.when(i + 1 < n_used)
        def _():
            issue_all(gather_copy, i + 1, 1 - slot)

        prev = be_ref[jnp.maximum(i - 1, 0)]

        @pl.when((i == 0) | (be_ref[i] != prev))
        def _():
            wgb[...] = wg_ref[0].astype(BF16)
            wub[...] = wu_ref[0].astype(BF16)
            wdb[...] = wd_ref[0].astype(BF16)

        x = _load_token_tiles(xbuf.at[slot], bm).astype(BF16)
        gate = jnp.dot(x, wgb[...], preferred_element_type=F32)
        up = jnp.dot(x, wub[...], preferred_element_type=F32)
        hid = (jax.nn.silu(gate) * up).astype(BF16)
        y = jnp.dot(hid, wdb[...], preferred_element_type=F32)

        @pl.when(i >= 2)
        def _():
            wait_scatter(slot)

        _store_token_tiles(ybuf.at[slot], y)
        issue_all(scatter_copy, i, slot)

        @pl.when(i == n_used - 1)
        def _():
            @pl.when(i >= 1)
            def _():
                wait_scatter(1 - slot)
            wait_scatter(slot)


def _moe_ffn(block_e, block_src, block_nv, n_used, order, h8, w_gate, w_up, w_down, *, bm):
    n_tok = h8.shape[0] // SUBLANES
    d = SUBLANES * LANES
    n_blocks = block_e.shape[0]
    de = w_gate.shape[2]
    out_rows = (2 * n_tok + 2 * bm) * SUBLANES
    wspec = lambda shape: pl.BlockSpec(shape, lambda i, be, *_: (be[i], 0, 0))
    grid_spec = pltpu.PrefetchScalarGridSpec(
        num_scalar_prefetch=5,
        grid=(n_blocks,),
        in_specs=[
            pl.BlockSpec(memory_space=pl.ANY),
            wspec((1, d, de)), wspec((1, d, de)), wspec((1, de, d)),
        ],
        out_specs=pl.BlockSpec(memory_space=pl.ANY),
        scratch_shapes=[
            pltpu.VMEM((2, bm * SUBLANES, LANES), F32),
            pltpu.VMEM((2, bm * SUBLANES, LANES), F32),
            pltpu.VMEM((d, de), BF16),
            pltpu.VMEM((d, de), BF16),
            pltpu.VMEM((de, d), BF16),
            pltpu.SemaphoreType.DMA((2,)),
            pltpu.SemaphoreType.DMA((2,)),
        ],
    )
    return pl.pallas_call(
        functools.partial(_moe_body, bm=bm, n_tok=n_tok),
        grid_spec=grid_spec,
        out_shape=jax.ShapeDtypeStruct((out_rows, LANES), F32),
        compiler_params=pltpu.CompilerParams(
            dimension_semantics=("arbitrary",), vmem_limit_bytes=VMEM_LIMIT),
        name="moe_ffn",
    )(block_e, block_src, block_nv, n_used, order, h8, w_gate, w_up, w_down)


def _combine_body(h8_ref, y0_ref, y1_ref, route_ref, g_ref, b_ref, o_ref, *, tt, alpha):
    route = route_ref[...]
    w0 = route[:, 2:3]
    w1 = route[:, 3:4]
    h = _load_token_tiles(h8_ref, tt)
    moe = w0 * _load_token_tiles(y0_ref, tt) + w1 * _load_token_tiles(y1_ref, tt)
    o_ref[...] = _layernorm(alpha * h + moe, g_ref[...], b_ref[...])


def _combine(h8, out2, route, ln_g, ln_b, *, alpha, tt):
    t = route.shape[0]
    d = SUBLANES * LANES
    n_tiles = t // tt
    tile_rows = pl.BlockSpec((tt * SUBLANES, LANES), lambda i: (i, 0))
    return pl.pallas_call(
        functools.partial(_combine_body, tt=tt, alpha=alpha),
        grid=(n_tiles,),
        in_specs=[
            tile_rows,
            tile_rows,
            pl.BlockSpec((tt * SUBLANES, LANES), lambda i: (n_tiles + i, 0)),
            pl.BlockSpec((tt, LANES), lambda i: (i, 0)),
            pl.BlockSpec((1, d), lambda i: (0, 0)),
            pl.BlockSpec((1, d), lambda i: (0, 0)),
        ],
        out_specs=pl.BlockSpec((tt, d), lambda i: (i, 0)),
        out_shape=jax.ShapeDtypeStruct((t, d), F32),
        compiler_params=pltpu.CompilerParams(
            dimension_semantics=("parallel",), vmem_limit_bytes=VMEM_LIMIT),
        name="moe_combine",
    )(h8, out2, out2, route, ln_g, ln_b)


def _moe_layout(route, *, bm):
    t = route.shape[0]
    tk = 2 * t
    flat_e = route[:, 0:2].astype(I32).reshape(-1)
    key = jnp.sort(flat_e * tk + jnp.arange(tk, dtype=I32))
    order = key % tk
    experts = jnp.arange(N_EXPERTS, dtype=I32)
    counts = jnp.sum((flat_e[:, None] == experts[None, :]).astype(I32), axis=0)
    starts = jnp.cumsum(counts) - counts
    pcounts = (counts + bm - 1) // bm * bm
    pends = jnp.cumsum(pcounts)
    pstarts = pends - pcounts
    n_blocks = (tk + N_EXPERTS * bm) // bm
    row0 = jnp.arange(n_blocks, dtype=I32) * bm
    block_e = jnp.minimum(
        jnp.sum((pends[None, :] <= row0[:, None]).astype(I32), axis=1), N_EXPERTS - 1)
    off = row0 - pstarts[block_e]
    block_src = starts[block_e] + off
    block_nv = jnp.clip(counts[block_e] - off, 0, bm)
    n_used = (pends[-1] // bm).astype(I32).reshape(1)
    return block_e, block_src.astype(I32), block_nv.astype(I32), n_used, order.astype(I32)


def _layer(x, w_in, lam_q1, lam_k1, lam_q2, lam_k2, subln_g, a_re, a_im, log_dt, b_re, b_im,
           c_re, c_im, d, w_glu, b_glu, w_out, ln1_g, ln1_b, w_rg, w_re, w_gate, w_up, w_down,
           ln2_g, ln2_b, *, layer_idx, alpha, ts, qb, tb, bm, tt):
    b, s, dm = x.shape
    lam_init = 0.8 - 0.6 * math.exp(-0.3 * layer_idx)

    q, k, v, u = _in_proj(x, w_in.astype(BF16), ts=ts)

    lam_params = jnp.stack([lam_q1, lam_k1, lam_q2, lam_k2]).astype(F32)
    a_out = _diff_attention(q, k, v, lam_params, subln_g.reshape(1, -1).astype(F32),
                            lam_init=lam_init, qb=qb)

    bbd, cbd, lre, lim = _ssm_params(a_re, a_im, log_dt, b_re, b_im, c_re, c_im)
    s_out = _ssm_glu(u, bbd, cbd, lre, lim, d.reshape(1, -1).astype(F32), w_glu.astype(BF16),
                     b_glu.reshape(1, -1).astype(F32), tb=tb)

    w_r = jnp.concatenate([w_rg, w_re], axis=1).astype(F32)
    w_r_hi = w_r.astype(BF16)
    w_r_lo = (w_r - w_r_hi.astype(F32)).astype(BF16)
    w_route = jnp.concatenate(
        [w_r_hi, w_r_lo, jnp.zeros((dm, LANES - 2 * N_ROUTE_COLS), BF16)], axis=1)
    h8, route = _out_proj(a_out, s_out, x, w_out.astype(BF16), ln1_g.reshape(1, -1),
                          ln1_b.reshape(1, -1), w_route, alpha=alpha, ts=ts)

    route = route.reshape(b * s, LANES)
    block_e, block_src, block_nv, n_used, order = _moe_layout(route, bm=bm)
    out2 = _moe_ffn(block_e, block_src, block_nv, n_used, order, h8, w_gate, w_up, w_down, bm=bm)
    out = _combine(h8, out2, route, ln2_g.reshape(1, -1), ln2_b.reshape(1, -1), alpha=alpha, tt=tt)
    return out.reshape(b, s, dm)


def kernel(x, w_in, lam_q1, lam_k1, lam_q2, lam_k2, subln_g, ssm_a_re, ssm_a_im, ssm_log_dt, ssm_b_re, ssm_b_im, ssm_c_re, ssm_c_im, ssm_d, w_glu, b_glu, w_out, ln1_g, ln1_b, w_router_group, w_router_expert, w_exp_gate, w_exp_up, w_exp_down, ln2_g, ln2_b):
    depth = w_in.shape[0]
    alpha = (2.0 * depth) ** 0.25
    for i in range(depth):
        x = _layer(
            x, w_in[i], lam_q1[i], lam_k1[i], lam_q2[i], lam_k2[i], subln_g[i], ssm_a_re[i],
            ssm_a_im[i], ssm_log_dt[i], ssm_b_re[i], ssm_b_im[i], ssm_c_re[i], ssm_c_im[i],
            ssm_d[i], w_glu[i], b_glu[i], w_out[i], ln1_g[i], ln1_b[i], w_router_group[i],
            w_router_expert[i], w_exp_gate[i], w_exp_up[i], w_exp_down[i], ln2_g[i], ln2_b[i],
            layer_idx=i, alpha=alpha, ts=512, qb=256, tb=256, bm=256, tt=256)
    return x
```

```python
import functools
import math

import jax
import jax.numpy as jnp
from jax import lax
from jax.experimental import pallas as pl
from jax.experimental.pallas import tpu as pltpu

F32 = jnp.float32
BF16 = jnp.bfloat16
I32 = jnp.int32

DEPTH = 1
N_HEADS = 4
HEAD_DIM = 64
ATTN_WIDTH = N_HEADS * 2 * HEAD_DIM
CHUNK = 64
GROUP_CH = 16
SSM_STATE = 64
N_EXPERT_GROUPS = 4
EXPERTS_PER_GROUP = 8
N_EXPERTS = N_EXPERT_GROUPS * EXPERTS_PER_GROUP
LN_EPS = 1e-5
RMS_EPS = 1e-5

LANES = 128
SUBLANES = 8
VMEM_LIMIT = 56 * 1024 * 1024

NEG_BIG = -1e30


def _in_proj_body(x_ref, w_ref, q_ref, k_ref, v_ref, u_ref, *, width, q_scale):
    x = x_ref[0].astype(BF16)

    def proj(c):
        return jnp.dot(x, w_ref[:, c * width:(c + 1) * width], preferred_element_type=F32)

    q_ref[0] = (proj(0) * q_scale).astype(BF16)
    k_ref[0] = proj(1).astype(BF16)
    v_ref[0] = proj(2).astype(BF16)
    u_ref[0] = proj(3)


def _in_proj(x, w_in_bf16, *, ts):
    b, s, d = x.shape
    width = ATTN_WIDTH
    out_block = pl.BlockSpec((1, ts, width), lambda bi, i: (bi, i, 0))
    return pl.pallas_call(
        functools.partial(_in_proj_body, width=width, q_scale=HEAD_DIM ** -0.5 * math.log2(math.e)),
        grid=(b, s // ts),
        in_specs=[
            pl.BlockSpec((1, ts, d), lambda bi, i: (bi, i, 0)),
            pl.BlockSpec((d, 4 * width), lambda bi, i: (0, 0)),
        ],
        out_specs=[out_block, out_block, out_block, out_block],
        out_shape=[
            jax.ShapeDtypeStruct((b, s, width), BF16),
            jax.ShapeDtypeStruct((b, s, width), BF16),
            jax.ShapeDtypeStruct((b, s, width), BF16),
            jax.ShapeDtypeStruct((b, s, width), F32),
        ],
        compiler_params=pltpu.CompilerParams(
            dimension_semantics=("parallel", "parallel"), vmem_limit_bytes=VMEM_LIMIT),
        name="in_proj",
    )(x, w_in_bf16)


def _attn_body(lamp_ref, g_ref, q_ref, k_ref, v_ref, o_ref, *, lam_init, n_qt, qb, kb):
    lp = lamp_ref[...]
    s1 = jnp.sum(lp[0:1] * lp[1:2], axis=-1, keepdims=True)
    s2 = jnp.sum(lp[2:3] * lp[3:4], axis=-1, keepdims=True)
    lam = jnp.exp(s1) - jnp.exp(s2) + lam_init

    lane = lax.broadcasted_iota(I32, (qb, 2 * HEAD_DIM), 1)
    row = lax.broadcasted_iota(I32, (2 * qb, kb), 0)
    col = lax.broadcasted_iota(I32, (2 * qb, kb), 1)
    row = jnp.where(row >= qb, row - qb, row)
    chunk_gap = col // CHUNK - row // CHUNK

    def q_tile(i, carry_unused):
        q = q_ref[0, pl.ds(pl.multiple_of(i * qb, qb), qb), :]
        zero = jnp.zeros_like(q)
        qq = jnp.concatenate(
            [jnp.where(lane < HEAD_DIM, q, zero), jnp.where(lane >= HEAD_DIM, q, zero)], axis=0)

        def kv_step(start, width, carry, allowed=None):
            m, l, acc = carry
            start = pl.multiple_of(start, qb)
            kblk = k_ref[0, pl.ds(start, width), :]
            vblk = v_ref[0, pl.ds(start, width), :]
            s = lax.dot_general(qq, kblk, (((1,), (1,)), ((), ())), preferred_element_type=F32)
            if allowed is not None:
                s = jnp.where(allowed, s, NEG_BIG)
            m_new = jnp.maximum(m, jnp.max(s, axis=-1, keepdims=True))
            p = jnp.exp2(s - m_new)
            a = jnp.exp2(m - m_new)
            l = a * l + jnp.sum(p, axis=-1, keepdims=True)
            acc = a * acc + jnp.dot(p.astype(BF16), vblk, preferred_element_type=F32)
            return m_new, l, acc

        carry = (jnp.full((2 * qb, 1), NEG_BIG, F32), jnp.zeros((2 * qb, 1), F32),
                 jnp.zeros((2 * qb, 2 * HEAD_DIM), F32))
        n_full = i // (kb // qb)

        def full_pair(jj, c):
            c = kv_step(2 * jj * kb, kb, c)
            return kv_step((2 * jj + 1) * kb, kb, c)

        carry = lax.fori_loop(0, n_full // 2, full_pair, carry)
        carry = lax.cond(n_full % 2 == 1,
                         lambda c: kv_step((n_full - 1) * kb, kb, c), lambda c: c, carry)
        _, l, acc = lax.cond(
            i % 2 == 0,
            lambda c: kv_step(i * qb, qb, c, chunk_gap[:, :qb] <= 0),
            lambda c: kv_step((i - 1) * qb, kb, c, chunk_gap <= qb // CHUNK),
            carry)
        o = acc / l
        od = o[:qb] - lam * o[qb:]
        ms = jnp.mean(od * od, axis=-1, keepdims=True)
        od = od * lax.rsqrt(ms + RMS_EPS) * g_ref[...] * (1.0 - lam_init)
        o_ref[0, pl.ds(pl.multiple_of(i * qb, qb), qb), :] = od.astype(BF16)
        return carry_unused

    lax.fori_loop(0, n_qt, q_tile, 0)


def _diff_attention(q, k, v, lam_params, subln_g, *, lam_init, qb):
    b, s, width = q.shape
    hw = 2 * HEAD_DIM
    kb = min(2 * qb, s)
    seq_block = pl.BlockSpec((1, s, hw), lambda bi, h: (bi, 0, h))
    return pl.pallas_call(
        functools.partial(_attn_body, lam_init=lam_init, n_qt=s // qb, qb=qb, kb=kb),
        grid=(b, N_HEADS),
        in_specs=[
            pl.BlockSpec((4, HEAD_DIM), lambda bi, h: (0, 0)),
            pl.BlockSpec((1, hw), lambda bi, h: (0, 0)),
            seq_block, seq_block, seq_block,
        ],
        out_specs=seq_block,
        out_shape=jax.ShapeDtypeStruct((b, s, width), BF16),
        compiler_params=pltpu.CompilerParams(
            dimension_semantics=("parallel", "parallel"), vmem_limit_bytes=VMEM_LIMIT),
        name="diff_attn",
    )(lam_params, subln_g, q, k, v)


SCAN_UNROLL = 4


def _ssm_body(u_ref, bbd_ref, cbd_ref, lre_ref, lim_ref, d_ref, wg_ref, bg_ref, z_ref,
              scr, hst, ysc, *, tb, tbp, n_lt):
    nb = u_ref.shape[0]
    n_half = 4

    @pl.when(pl.program_id(0) == 0)
    def _():
        hst[...] = jnp.zeros_like(hst)

    for lt in range(n_lt):
        lanes = slice(lt * LANES, (lt + 1) * LANES)
        for b in range(nb):
            ub = u_ref[b, :, lanes].astype(BF16)
            bu = jnp.dot(ub, bbd_ref[lt], preferred_element_type=F32)
            for k in range(2 * n_half):
                scr[k, b * tbp:b * tbp + tb, :] = bu[:, k * LANES:(k + 1) * LANES]

        lr = [jnp.broadcast_to(lre_ref[lt, :, k * LANES:(k + 1) * LANES], (nb, LANES))
              for k in range(n_half)]
        li = [jnp.broadcast_to(lim_ref[lt, :, k * LANES:(k + 1) * LANES], (nb, LANES))
              for k in range(n_half)]

        def step(t, h):
            new_re, new_im = [], []
            for k in range(n_half):
                rows = pl.ds(t, nb, stride=tbp)
                hr, hi = h[k], h[n_half + k]
                nr = lr[k] * hr - li[k] * hi + scr[k, rows, :]
                ni = lr[k] * hi + li[k] * hr + scr[n_half + k, rows, :]
                scr[k, rows, :] = nr
                scr[n_half + k, rows, :] = ni
                new_re.append(nr)
                new_im.append(ni)
            return tuple(new_re + new_im)

        h = lax.fori_loop(0, tb, step, tuple(hst[lt, k] for k in range(2 * n_half)),
                          unroll=SCAN_UNROLL)
        for k in range(2 * n_half):
            hst[lt, k] = h[k]

        for b in range(nb):
            hb = jnp.concatenate(
                [scr[k, b * tbp:b * tbp + tb, :] for k in range(2 * n_half)], axis=1)
            yb = jnp.dot(hb.astype(BF16), cbd_ref[lt], preferred_element_type=F32)
            ysc[b, :, lanes] = yb + d_ref[:, lanes] * u_ref[b, :, lanes]

    for b in range(nb):
        z = jax.nn.gelu(ysc[b])
        gate = jax.nn.sigmoid(
            jnp.dot(z.astype(BF16), wg_ref[...], preferred_element_type=F32) + bg_ref[...])
        z_ref[b] = (z * gate).astype(BF16)


def _ssm_params(a_re, a_im, log_dt, b_re, b_im, c_re, c_im):
    g = a_re.shape[0]
    n_lt = g * GROUP_CH // LANES
    gpt = g // n_lt
    dt = jnp.exp(log_dt.astype(F32))[:, None]
    ar, ai = a_re.astype(F32), a_im.astype(F32)
    mag = jnp.exp(ar * dt)
    lam_re = mag * jnp.cos(ai * dt)
    lam_im = mag * jnp.sin(ai * dt)
    nr, ni = lam_re - 1.0, lam_im
    den = ar * ar + ai * ai
    cr = (nr * ar + ni * ai) / den
    ci = (ni * ar - nr * ai) / den
    bb_re = cr[..., None] * b_re - ci[..., None] * b_im
    bb_im = cr[..., None] * b_im + ci[..., None] * b_re
    eye = jnp.eye(gpt, dtype=F32)

    def pack_b(bb):
        bb = bb.reshape(n_lt, gpt, SSM_STATE, GROUP_CH)
        m = jnp.einsum('lgpc,gh->lgchp', bb, eye)
        return m.reshape(n_lt, gpt * GROUP_CH, gpt * SSM_STATE)

    def pack_c(cc):
        cc = cc.astype(F32).reshape(n_lt, gpt, GROUP_CH, SSM_STATE)
        m = jnp.einsum('lgcp,gh->lgphc', cc, eye)
        return m.reshape(n_lt, gpt * SSM_STATE, gpt * GROUP_CH)

    bbd = jnp.concatenate([pack_b(bb_re), pack_b(bb_im)], axis=2).astype(BF16)
    cbd = jnp.concatenate([pack_c(c_re), -pack_c(c_im)], axis=1).astype(BF16)
    lre = lam_re.reshape(n_lt, 1, gpt * SSM_STATE)
    lim = lam_im.reshape(n_lt, 1, gpt * SSM_STATE)
    return bbd, cbd, lre, lim


def _ssm_glu(u, bbd, cbd, lre, lim, d, w_glu_bf16, b_glu, *, tb):
    b, s, width = u.shape
    assert b == SUBLANES
    n_lt = width // LANES
    tbp = tb + SUBLANES
    full3 = lambda a: pl.BlockSpec(a.shape, lambda i: (0, 0, 0))
    full2 = lambda a: pl.BlockSpec(a.shape, lambda i: (0, 0))
    return pl.pallas_call(
        functools.partial(_ssm_body, tb=tb, tbp=tbp, n_lt=n_lt),
        grid=(s // tb,),
        in_specs=[
            pl.BlockSpec((b, tb, width), lambda i: (0, i, 0)),
            full3(bbd), full3(cbd), full3(lre), full3(lim), full2(d), full2(w_glu_bf16),
            full2(b_glu),
        ],
        out_specs=pl.BlockSpec((b, tb, width), lambda i: (0, i, 0)),
        out_shape=jax.ShapeDtypeStruct((b, s, width), BF16),
        scratch_shapes=[
            pltpu.VMEM((8, b * tbp, LANES), F32),
            pltpu.VMEM((n_lt, 8, b, LANES), F32),
            pltpu.VMEM((b, tb, width), F32),
        ],
        compiler_params=pltpu.CompilerParams(
            dimension_semantics=("arbitrary",), vmem_limit_bytes=VMEM_LIMIT),
        name="ssm_glu",
    )(u, bbd, cbd, lre, lim, d, w_glu_bf16, b_glu)


def _layernorm(y, g, b):
    mu = jnp.mean(y, axis=-1, keepdims=True)
    yc = y - mu
    var = jnp.mean(yc * yc, axis=-1, keepdims=True)
    return yc * lax.rsqrt(var + LN_EPS) * g + b


N_ROUTE_COLS = N_EXPERT_GROUPS + N_EXPERTS


def _store_token_tiles(ref, val):
    n = val.shape[0]
    for j in range(SUBLANES):
        ref[pl.ds(j, n, stride=SUBLANES), :] = val[:, j * LANES:(j + 1) * LANES]


def _load_token_tiles(ref, n):
    return jnp.concatenate(
        [ref[pl.ds(j, n, stride=SUBLANES), :] for j in range(SUBLANES)], axis=1)


def _out_proj_body(a_ref, z_ref, x_ref, wo_ref, g_ref, b_ref, wr_ref, h8_ref, route_ref, *, alpha):
    aw = a_ref.shape[2]
    mix = jnp.dot(a_ref[0], wo_ref[0:aw, :], preferred_element_type=F32)
    mix += jnp.dot(z_ref[0], wo_ref[aw:, :], preferred_element_type=F32)
    h = _layernorm(alpha * x_ref[0] + mix, g_ref[...], b_ref[...])
    _store_token_tiles(h8_ref, h)

    h_hi = h.astype(BF16)
    h_lo = (h - h_hi.astype(F32)).astype(BF16)
    lg = jnp.dot(h_hi, wr_ref[...], preferred_element_type=F32)
    lg += jnp.dot(h_lo, wr_ref[...], preferred_element_type=F32)
    lg = lg + pltpu.roll(lg, LANES - N_ROUTE_COLS, 1)

    lane = lax.broadcasted_iota(I32, lg.shape, 1)

    def argmax_lane(vals):
        mx = jnp.max(vals, axis=-1, keepdims=True)
        idx = jnp.min(jnp.where(vals == mx, lane, LANES), axis=-1, keepdims=True)
        return mx, idx

    gl = jnp.where(lane < N_EXPERT_GROUPS, lg, NEG_BIG)
    g_max, g_idx = argmax_lane(gl)
    g_p = 1.0 / jnp.sum(jnp.exp(gl - g_max), axis=-1, keepdims=True)

    e_lo = N_EXPERT_GROUPS + g_idx * EXPERTS_PER_GROUP
    el = jnp.where((lane >= e_lo) & (lane < e_lo + EXPERTS_PER_GROUP), lg, NEG_BIG)
    e1, i1 = argmax_lane(el)
    e2, i2 = argmax_lane(jnp.where(lane == i1, NEG_BIG, el))
    r = jnp.exp(e2 - e1)
    w1 = g_p / (1.0 + r)
    w2 = g_p * r / (1.0 + r)
    route = jnp.where(lane == 0, (i1 - N_EXPERT_GROUPS).astype(F32), 0.0)
    route = jnp.where(lane == 1, (i2 - N_EXPERT_GROUPS).astype(F32), route)
    route = jnp.where(lane == 2, w1, route)
    route = jnp.where(lane == 3, w2, route)
    route_ref[0] = route


def _out_proj(a, z, x, w_out_bf16, ln_g, ln_b, w_route, *, alpha, ts):
    b, s, d = x.shape
    aw = a.shape[2]
    half = pl.BlockSpec((1, ts, aw), lambda bi, i: (bi, i, 0))
    row = pl.BlockSpec((1, ts, d), lambda bi, i: (bi, i, 0))
    full2 = lambda arr: pl.BlockSpec(arr.shape, lambda bi, i: (0, 0))
    assert d == SUBLANES * LANES
    n_st = s // ts
    return pl.pallas_call(
        functools.partial(_out_proj_body, alpha=alpha),
        grid=(b, n_st),
        in_specs=[half, half, row, full2(w_out_bf16), full2(ln_g), full2(ln_b), full2(w_route)],
        out_specs=[
            pl.BlockSpec((ts * SUBLANES, LANES), lambda bi, i: (bi * n_st + i, 0)),
            pl.BlockSpec((1, ts, LANES), lambda bi, i: (bi, i, 0)),
        ],
        out_shape=[
            jax.ShapeDtypeStruct((b * s * SUBLANES, LANES), F32),
            jax.ShapeDtypeStruct((b, s, LANES), F32),
        ],
        compiler_params=pltpu.CompilerParams(
            dimension_semantics=("parallel", "parallel"), vmem_limit_bytes=VMEM_LIMIT),
        name="out_proj_router",
    )(a, z, x, w_out_bf16, ln_g, ln_b, w_route)


ISSUE_UNROLL = 8


def _moe_body(be_ref, src_ref, nv_ref, nused_ref, order_ref, h8_hbm, wg_ref, wu_ref, wd_ref,
              out_hbm, xbuf, ybuf, wgb, wub, wdb, gsem, ssem, *, bm, n_tok):
    i = pl.program_id(0)
    n_used = nused_ref[0]
    n_pairs = 2 * n_tok
    tile = SUBLANES

    def tile_row(r):
        return r * tile if isinstance(r, int) else pl.multiple_of(r * tile, tile)

    def gather_copy(blk, buf_slot, r):
        p = order_ref[jnp.minimum(src_ref[blk] + r, n_pairs - 1)]
        tok = p >> 1
        return pltpu.make_async_copy(
            h8_hbm.at[pl.ds(pl.multiple_of(tok * tile, tile), tile)],
            xbuf.at[buf_slot, pl.ds(tile_row(r), tile)],
            gsem.at[buf_slot])

    def scatter_copy(blk, buf_slot, r):
        blk_c = jnp.maximum(blk, 0)
        n_valid = jnp.where(blk >= 0, nv_ref[blk_c], 0)
        p = order_ref[jnp.minimum(src_ref[blk_c] + r, n_pairs - 1)]
        real = (p & 1) * n_tok + (p >> 1)
        pad = n_pairs + buf_slot * bm + r
        dst = jnp.where(r < n_valid, real, pad)
        return pltpu.make_async_copy(
            ybuf.at[buf_slot, pl.ds(tile_row(r), tile)],
            out_hbm.at[pl.ds(pl.multiple_of(dst * tile, tile), tile)],
            ssem.at[buf_slot])

    def issue_loop(make_copy, blk, buf_slot):
        def trip(g, c):
            for q in range(ISSUE_UNROLL):
                make_copy(blk, buf_slot, pl.multiple_of(g * ISSUE_UNROLL, ISSUE_UNROLL) + q).start()
            return c
        lax.fori_loop(0, bm // ISSUE_UNROLL, trip, 0)

    def wait_gather(buf_slot):
        pltpu.make_async_copy(h8_hbm.at[pl.ds(0, bm * tile)], xbuf.at[buf_slot],
                              gsem.at[buf_slot]).wait()

    def wait_scatter(buf_slot):
        pltpu.make_async_copy(ybuf.at[buf_slot], out_hbm.at[pl.ds(0, bm * tile)],
                              ssem.at[buf_slot]).wait()

    @pl.when(i == 0)
    def _():
        issue_loop(gather_copy, 0, 0)
        ybuf[...] = jnp.zeros_like(ybuf)
        for s in range(2):
            pad_rows = out_hbm.at[pl.ds((n_pairs + s * bm) * tile, bm * tile)]
            pltpu.make_async_copy(ybuf.at[s], pad_rows, ssem.at[s]).start()
            pltpu.make_async_copy(ybuf.at[s], pad_rows, ssem.at[s]).wait()

    def block_step(slot):
        other = 1 - slot
        wait_gather(slot)

        @pl.when(i >= 1)
        def _():
            wait_scatter(slot)

        prev = be_ref[jnp.maximum(i - 1, 0)]

        @pl.when((i == 0) | (be_ref[i] != prev))
        def _():
            wgb[...] = wg_ref[0].astype(BF16)
            wub[...] = wu_ref[0].astype(BF16)
            wdb[...] = wd_ref[0].astype(BF16)

        x = _load_token_tiles(xbuf.at[slot], bm).astype(BF16)
        nxt = jnp.minimum(i + 1, n_used - 1)
        for r in range(bm):
            gather_copy(nxt, other, r).start()
            scatter_copy(i - 1, other, r).start()
        gate = jnp.dot(x, wgb[...], preferred_element_type=F32)
        up = jnp.dot(x, wub[...], preferred_element_type=F32)
        hid = (jax.nn.silu(gate) * up).astype(BF16)
        y = jnp.dot(hid, wdb[...], preferred_element_type=F32)
        _store_token_tiles(ybuf.at[slot], y)

    for s in range(2):
        pl.when((i < n_used) & (i % 2 == s))(functools.partial(block_step, s))

    @pl.when(i == n_used)
    def _():
        last_slot = (n_used - 1) % 2
        issue_loop(scatter_copy, n_used - 1, last_slot)
        wait_scatter(0)
        wait_scatter(1)
        pltpu.make_async_copy(h8_hbm.at[pl.ds(0, bm * tile)], xbuf.at[1 - last_slot],
                              gsem.at[1 - last_slot]).wait()


def _moe_ffn(block_e, block_src, block_nv, n_used, order, h8, w_gate, w_up, w_down, *, bm):
    n_tok = h8.shape[0] // SUBLANES
    d = SUBLANES * LANES
    n_blocks = block_e.shape[0]
    de = w_gate.shape[2]
    out_rows = (2 * n_tok + 2 * bm) * SUBLANES
    wspec = lambda shape: pl.BlockSpec(shape, lambda i, be, *_: (be[i], 0, 0))
    grid_spec = pltpu.PrefetchScalarGridSpec(
        num_scalar_prefetch=5,
        grid=(n_blocks,),
        in_specs=[
            pl.BlockSpec(memory_space=pl.ANY),
            wspec((1, d, de)), wspec((1, d, de)), wspec((1, de, d)),
        ],
        out_specs=pl.BlockSpec(memory_space=pl.ANY),
        scratch_shapes=[
            pltpu.VMEM((2, bm * SUBLANES, LANES), F32),
            pltpu.VMEM((2, bm * SUBLANES, LANES), F32),
            pltpu.VMEM((d, de), BF16),
            pltpu.VMEM((d, de), BF16),
            pltpu.VMEM((de, d), BF16),
            pltpu.SemaphoreType.DMA((2,)),
            pltpu.SemaphoreType.DMA((2,)),
        ],
    )
    return pl.pallas_call(
        functools.partial(_moe_body, bm=bm, n_tok=n_tok),
        grid_spec=grid_spec,
        out_shape=jax.ShapeDtypeStruct((out_rows, LANES), F32),
        compiler_params=pltpu.CompilerParams(
            dimension_semantics=("arbitrary",), vmem_limit_bytes=VMEM_LIMIT),
        name="moe_ffn",
    )(block_e, block_src, block_nv, n_used, order, h8, w_gate, w_up, w_down)


def _combine_body(h8_ref, y0_ref, y1_ref, route_ref, g_ref, b_ref, o_ref, *, tt, alpha):
    route = route_ref[...]
    w0 = route[:, 2:3]
    w1 = route[:, 3:4]
    h = _load_token_tiles(h8_ref, tt)
    moe = w0 * _load_token_tiles(y0_ref, tt) + w1 * _load_token_tiles(y1_ref, tt)
    o_ref[...] = _layernorm(alpha * h + moe, g_ref[...], b_ref[...])


def _combine(h8, out2, route, ln_g, ln_b, *, alpha, tt):
    t = route.shape[0]
    d = SUBLANES * LANES
    n_tiles = t // tt
    tile_rows = pl.BlockSpec((tt * SUBLANES, LANES), lambda i: (i, 0))
    return pl.pallas_call(
        functools.partial(_combine_body, tt=tt, alpha=alpha),
        grid=(n_tiles,),
        in_specs=[
            tile_rows,
            tile_rows,
            pl.BlockSpec((tt * SUBLANES, LANES), lambda i: (n_tiles + i, 0)),
            pl.BlockSpec((tt, LANES), lambda i: (i, 0)),
            pl.BlockSpec((1, d), lambda i: (0, 0)),
            pl.BlockSpec((1, d), lambda i: (0, 0)),
        ],
        out_specs=pl.BlockSpec((tt, d), lambda i: (i, 0)),
        out_shape=jax.ShapeDtypeStruct((t, d), F32),
        compiler_params=pltpu.CompilerParams(
            dimension_semantics=("parallel",), vmem_limit_bytes=VMEM_LIMIT),
        name="moe_combine",
    )(h8, out2, out2, route, ln_g, ln_b)


def _moe_layout(route, *, bm):
    t = route.shape[0]
    tk = 2 * t
    flat_e = route[:, 0:2].astype(I32).reshape(-1)
    key = jnp.sort(flat_e * tk + jnp.arange(tk, dtype=I32))
    order = key % tk
    experts = jnp.arange(N_EXPERTS, dtype=I32)
    counts = jnp.sum((flat_e[:, None] == experts[None, :]).astype(I32), axis=0)
    starts = jnp.cumsum(counts) - counts
    pcounts = (counts + bm - 1) // bm * bm
    pends = jnp.cumsum(pcounts)
    pstarts = pends - pcounts
    n_blocks = (tk + N_EXPERTS * bm) // bm
    row0 = jnp.arange(n_blocks, dtype=I32) * bm
    block_e = jnp.minimum(
        jnp.sum((pends[None, :] <= row0[:, None]).astype(I32), axis=1), N_EXPERTS - 1)
    off = row0 - pstarts[block_e]
    block_src = starts[block_e] + off
    block_nv = jnp.clip(counts[block_e] - off, 0, bm)
    n_used = (pends[-1] // bm).astype(I32).reshape(1)
    return block_e, block_src.astype(I32), block_nv.astype(I32), n_used, order.astype(I32)


def _layer(x, w_in, lam_q1, lam_k1, lam_q2, lam_k2, subln_g, a_re, a_im, log_dt, b_re, b_im,
           c_re, c_im, d, w_glu, b_glu, w_out, ln1_g, ln1_b, w_rg, w_re, w_gate, w_up, w_down,
           ln2_g, ln2_b, *, layer_idx, alpha, ts, qb, tb, bm, tt):
    b, s, dm = x.shape
    lam_init = 0.8 - 0.6 * math.exp(-0.3 * layer_idx)

    q, k, v, u = _in_proj(x, w_in.astype(BF16), ts=ts)

    lam_params = jnp.stack([lam_q1, lam_k1, lam_q2, lam_k2]).astype(F32)
    a_out = _diff_attention(q, k, v, lam_params, subln_g.reshape(1, -1).astype(F32),
                            lam_init=lam_init, qb=qb)

    bbd, cbd, lre, lim = _ssm_params(a_re, a_im, log_dt, b_re, b_im, c_re, c_im)
    s_out = _ssm_glu(u, bbd, cbd, lre, lim, d.reshape(1, -1).astype(F32), w_glu.astype(BF16),
                     b_glu.reshape(1, -1).astype(F32), tb=tb)

    w_r = jnp.concatenate([w_rg, w_re], axis=1).astype(F32)
    w_r_hi = w_r.astype(BF16)
    w_r_lo = (w_r - w_r_hi.astype(F32)).astype(BF16)
    w_route = jnp.concatenate(
        [w_r_hi, w_r_lo, jnp.zeros((dm, LANES - 2 * N_ROUTE_COLS), BF16)], axis=1)
    h8, route = _out_proj(a_out, s_out, x, w_out.astype(BF16), ln1_g.reshape(1, -1),
                          ln1_b.reshape(1, -1), w_route, alpha=alpha, ts=ts)

    route = route.reshape(b * s, LANES)
    block_e, block_src, block_nv, n_used, order = _moe_layout(route, bm=bm)
    out2 = _moe_ffn(block_e, block_src, block_nv, n_used, order, h8, w_gate, w_up, w_down, bm=bm)
    out = _combine(h8, out2, route, ln2_g.reshape(1, -1), ln2_b.reshape(1, -1), alpha=alpha, tt=tt)
    return out.reshape(b, s, dm)


def kernel(x, w_in, lam_q1, lam_k1, lam_q2, lam_k2, subln_g, ssm_a_re, ssm_a_im, ssm_log_dt, ssm_b_re, ssm_b_im, ssm_c_re, ssm_c_im, ssm_d, w_glu, b_glu, w_out, ln1_g, ln1_b, w_router_group, w_router_expert, w_exp_gate, w_exp_up, w_exp_down, ln2_g, ln2_b):
    depth = w_in.shape[0]
    alpha = (2.0 * depth) ** 0.25
    for i in range(depth):
        x = _layer(
            x, w_in[i], lam_q1[i], lam_k1[i], lam_q2[i], lam_k2[i], subln_g[i], ssm_a_re[i],
            ssm_a_im[i], ssm_log_dt[i], ssm_b_re[i], ssm_b_im[i], ssm_c_re[i], ssm_c_im[i],
            ssm_d[i], w_glu[i], b_glu[i], w_out[i], ln1_g[i], ln1_b[i], w_router_group[i],
            w_router_expert[i], w_exp_gate[i], w_exp_up[i], w_exp_down[i], ln2_g[i], ln2_b[i],
            layer_idx=i, alpha=alpha, ts=512, qb=256, tb=256, bm=256, tt=256)
    return x
```

```python
import functools
import math

import jax
import jax.numpy as jnp
from jax import lax
from jax.experimental import pallas as pl
from jax.experimental.pallas import tpu as pltpu

F32 = jnp.float32
BF16 = jnp.bfloat16
I32 = jnp.int32

DEPTH = 1
N_HEADS = 4
HEAD_DIM = 64
ATTN_WIDTH = N_HEADS * 2 * HEAD_DIM
CHUNK = 64
GROUP_CH = 16
SSM_STATE = 64
N_EXPERT_GROUPS = 4
EXPERTS_PER_GROUP = 8
N_EXPERTS = N_EXPERT_GROUPS * EXPERTS_PER_GROUP
LN_EPS = 1e-5
RMS_EPS = 1e-5

LANES = 128
SUBLANES = 8
VMEM_LIMIT = 56 * 1024 * 1024

NEG_BIG = -1e30


def _in_proj_body(x_ref, w_ref, q_ref, k_ref, v_ref, u_ref, *, width, q_scale):
    x = x_ref[0].astype(BF16)

    def proj(c):
        return jnp.dot(x, w_ref[:, c * width:(c + 1) * width], preferred_element_type=F32)

    q_ref[0] = (proj(0) * q_scale).astype(BF16)
    k_ref[0] = proj(1).astype(BF16)
    v_ref[0] = proj(2).astype(BF16)
    u_ref[0] = proj(3)


def _in_proj(x, w_in_bf16, *, ts):
    b, s, d = x.shape
    width = ATTN_WIDTH
    out_block = pl.BlockSpec((1, ts, width), lambda bi, i: (bi, i, 0))
    return pl.pallas_call(
        functools.partial(_in_proj_body, width=width, q_scale=HEAD_DIM ** -0.5 * math.log2(math.e)),
        grid=(b, s // ts),
        in_specs=[
            pl.BlockSpec((1, ts, d), lambda bi, i: (bi, i, 0)),
            pl.BlockSpec((d, 4 * width), lambda bi, i: (0, 0)),
        ],
        out_specs=[out_block, out_block, out_block, out_block],
        out_shape=[
            jax.ShapeDtypeStruct((b, s, width), BF16),
            jax.ShapeDtypeStruct((b, s, width), BF16),
            jax.ShapeDtypeStruct((b, s, width), BF16),
            jax.ShapeDtypeStruct((b, s, width), F32),
        ],
        compiler_params=pltpu.CompilerParams(
            dimension_semantics=("parallel", "parallel"), vmem_limit_bytes=VMEM_LIMIT),
        name="in_proj",
    )(x, w_in_bf16)


ATTN_ROWS = 32


def _attn_body(lamp_ref, g_ref, q_ref, k_ref, v_ref, o_ref,
               qq_scr, m_scr, l_scr, acc_scr, *, lam_init, n_qt, qb, kb):
    lp = lamp_ref[...]
    s1 = jnp.sum(lp[0:1] * lp[1:2], axis=-1, keepdims=True)
    s2 = jnp.sum(lp[2:3] * lp[3:4], axis=-1, keepdims=True)
    lam = jnp.exp(s1) - jnp.exp(s2) + lam_init

    lane = lax.broadcasted_iota(I32, (qb, 2 * HEAD_DIM), 1)
    n_sub = 2 * qb // ATTN_ROWS

    def kv_step(start, width, max_gap=None):
        start = pl.multiple_of(start, qb)
        kblk = k_ref[0, pl.ds(start, width), :]
        s = lax.dot_general(
            qq_scr[...], kblk, (((1,), (1,)), ((), ())), preferred_element_type=F32)
        if max_gap is not None:
            q_row = lax.broadcasted_iota(I32, s.shape, 0) % qb
            k_col = lax.broadcasted_iota(I32, s.shape, 1)
            s = jnp.where(k_col // CHUNK - q_row // CHUNK <= max_gap, s, NEG_BIG)
        m_old = m_scr[...]
        m_new = jnp.maximum(m_old, jnp.max(s, axis=-1, keepdims=True))
        a = jnp.exp2(m_old - m_new)
        m_scr[...] = m_new
        l_parts, p_parts = [], []
        for c in range(n_sub):
            rows = slice(c * ATTN_ROWS, (c + 1) * ATTN_ROWS)
            m_c = m_new[rows]
            cols = [jnp.exp2(s[rows, j * LANES:(j + 1) * LANES] - m_c) for j in range(width // LANES)]
            l_parts.append(functools.reduce(lambda x, y: x + y, cols))
            p_parts.append(jnp.concatenate(cols, axis=1).astype(BF16))
        l_scr[...] = a * l_scr[...] + jnp.concatenate(l_parts, axis=0)
        vblk = v_ref[0, pl.ds(start, width), :]
        pv = jnp.dot(jnp.concatenate(p_parts, axis=0), vblk, preferred_element_type=F32)
        acc_scr[...] = a * acc_scr[...] + pv

    def q_tile(i, carry_unused):
        q = q_ref[0, pl.ds(pl.multiple_of(i * qb, qb), qb), :]
        zero = jnp.zeros_like(q)
        qq_scr[0:qb, :] = jnp.where(lane < HEAD_DIM, q, zero)
        qq_scr[qb:2 * qb, :] = jnp.where(lane >= HEAD_DIM, q, zero)
        m_scr[...] = jnp.full(m_scr.shape, NEG_BIG, F32)
        l_scr[...] = jnp.zeros(l_scr.shape, F32)
        acc_scr[...] = jnp.zeros(acc_scr.shape, F32)

        n_full = i // (kb // qb)

        def full_pair(jj, c):
            kv_step(2 * jj * kb, kb)
            kv_step((2 * jj + 1) * kb, kb)
            return c

        lax.fori_loop(0, n_full // 2, full_pair, 0)

        @pl.when(n_full % 2 == 1)
        def _():
            kv_step((n_full - 1) * kb, kb)

        @pl.when(i % 2 == 0)
        def _():
            kv_step(i * qb, qb, max_gap=0)

        @pl.when(i % 2 == 1)
        def _():
            kv_step((i - 1) * qb, kb, max_gap=qb // CHUNK)

        o = acc_scr[...] / jnp.sum(l_scr[...], axis=-1, keepdims=True)
        od = o[:qb] - lam * o[qb:]
        ms = jnp.mean(od * od, axis=-1, keepdims=True)
        od = od * lax.rsqrt(ms + RMS_EPS) * g_ref[...] * (1.0 - lam_init)
        o_ref[0, pl.ds(pl.multiple_of(i * qb, qb), qb), :] = od.astype(BF16)
        return carry_unused

    lax.fori_loop(0, n_qt, q_tile, 0)


def _diff_attention(q, k, v, lam_params, subln_g, *, lam_init, qb):
    b, s, width = q.shape
    hw = 2 * HEAD_DIM
    kb = min(2 * qb, s)
    seq_block = pl.BlockSpec((1, s, hw), lambda bi, h: (bi, 0, h))
    return pl.pallas_call(
        functools.partial(_attn_body, lam_init=lam_init, n_qt=s // qb, qb=qb, kb=kb),
        grid=(b, N_HEADS),
        in_specs=[
            pl.BlockSpec((4, HEAD_DIM), lambda bi, h: (0, 0)),
            pl.BlockSpec((1, hw), lambda bi, h: (0, 0)),
            seq_block, seq_block, seq_block,
        ],
        out_specs=seq_block,
        out_shape=jax.ShapeDtypeStruct((b, s, width), BF16),
        scratch_shapes=[
            pltpu.VMEM((2 * qb, hw), BF16),
            pltpu.VMEM((2 * qb, LANES), F32),
            pltpu.VMEM((2 * qb, LANES), F32),
            pltpu.VMEM((2 * qb, hw), F32),
        ],
        compiler_params=pltpu.CompilerParams(
            dimension_semantics=("parallel", "parallel"), vmem_limit_bytes=VMEM_LIMIT),
        name="diff_attn",
    )(lam_params, subln_g, q, k, v)


SCAN_UNROLL = 4


def _ssm_body(u_ref, bbd_ref, cbd_ref, lre_ref, lim_ref, d_ref, wg_ref, bg_ref, z_ref,
              scr, hst, ysc, *, tb, tbp, n_lt):
    nb = u_ref.shape[0]
    n_half = 4

    @pl.when(pl.program_id(0) == 0)
    def _():
        hst[...] = jnp.zeros_like(hst)

    for lt in range(n_lt):
        lanes = slice(lt * LANES, (lt + 1) * LANES)
        for b in range(nb):
            ub = u_ref[b, :, lanes].astype(BF16)
            bu = jnp.dot(ub, bbd_ref[lt], preferred_element_type=F32)
            for k in range(2 * n_half):
                scr[k, b * tbp:b * tbp + tb, :] = bu[:, k * LANES:(k + 1) * LANES]

        lr = [jnp.broadcast_to(lre_ref[lt, :, k * LANES:(k + 1) * LANES], (nb, LANES))
              for k in range(n_half)]
        li = [jnp.broadcast_to(lim_ref[lt, :, k * LANES:(k + 1) * LANES], (nb, LANES))
              for k in range(n_half)]

        def step(t, h):
            new_re, new_im = [], []
            for k in range(n_half):
                rows = pl.ds(t, nb, stride=tbp)
                hr, hi = h[k], h[n_half + k]
                nr = lr[k] * hr - li[k] * hi + scr[k, rows, :]
                ni = lr[k] * hi + li[k] * hr + scr[n_half + k, rows, :]
                scr[k, rows, :] = nr
                scr[n_half + k, rows, :] = ni
                new_re.append(nr)
                new_im.append(ni)
            return tuple(new_re + new_im)

        h = lax.fori_loop(0, tb, step, tuple(hst[lt, k] for k in range(2 * n_half)),
                          unroll=SCAN_UNROLL)
        for k in range(2 * n_half):
            hst[lt, k] = h[k]

        for b in range(nb):
            hb = jnp.concatenate(
                [scr[k, b * tbp:b * tbp + tb, :] for k in range(2 * n_half)], axis=1)
            yb = jnp.dot(hb.astype(BF16), cbd_ref[lt], preferred_element_type=F32)
            ysc[b, :, lanes] = yb + d_ref[:, lanes] * u_ref[b, :, lanes]

    for b in range(nb):
        z = jax.nn.gelu(ysc[b])
        gate = jax.nn.sigmoid(
            jnp.dot(z.astype(BF16), wg_ref[...], preferred_element_type=F32) + bg_ref[...])
        z_ref[b] = (z * gate).astype(BF16)


def _ssm_params(a_re, a_im, log_dt, b_re, b_im, c_re, c_im):
    g = a_re.shape[0]
    n_lt = g * GROUP_CH // LANES
    gpt = g // n_lt
    dt = jnp.exp(log_dt.astype(F32))[:, None]
    ar, ai = a_re.astype(F32), a_im.astype(F32)
    mag = jnp.exp(ar * dt)
    lam_re = mag * jnp.cos(ai * dt)
    lam_im = mag * jnp.sin(ai * dt)
    nr, ni = lam_re - 1.0, lam_im
    den = ar * ar + ai * ai
    cr = (nr * ar + ni * ai) / den
    ci = (ni * ar - nr * ai) / den
    bb_re = cr[..., None] * b_re - ci[..., None] * b_im
    bb_im = cr[..., None] * b_im + ci[..., None] * b_re
    eye = jnp.eye(gpt, dtype=F32)

    def pack_b(bb):
        bb = bb.reshape(n_lt, gpt, SSM_STATE, GROUP_CH)
        m = jnp.einsum('lgpc,gh->lgchp', bb, eye)
        return m.reshape(n_lt, gpt * GROUP_CH, gpt * SSM_STATE)

    def pack_c(cc):
        cc = cc.astype(F32).reshape(n_lt, gpt, GROUP_CH, SSM_STATE)
        m = jnp.einsum('lgcp,gh->lgphc', cc, eye)
        return m.reshape(n_lt, gpt * SSM_STATE, gpt * GROUP_CH)

    bbd = jnp.concatenate([pack_b(bb_re), pack_b(bb_im)], axis=2).astype(BF16)
    cbd = jnp.concatenate([pack_c(c_re), -pack_c(c_im)], axis=1).astype(BF16)
    lre = lam_re.reshape(n_lt, 1, gpt * SSM_STATE)
    lim = lam_im.reshape(n_lt, 1, gpt * SSM_STATE)
    return bbd, cbd, lre, lim


def _ssm_glu(u, bbd, cbd, lre, lim, d, w_glu_bf16, b_glu, *, tb):
    b, s, width = u.shape
    assert b == SUBLANES
    n_lt = width // LANES
    tbp = tb + SUBLANES
    full3 = lambda a: pl.BlockSpec(a.shape, lambda i: (0, 0, 0))
    full2 = lambda a: pl.BlockSpec(a.shape, lambda i: (0, 0))
    return pl.pallas_call(
        functools.partial(_ssm_body, tb=tb, tbp=tbp, n_lt=n_lt),
        grid=(s // tb,),
        in_specs=[
            pl.BlockSpec((b, tb, width), lambda i: (0, i, 0)),
            full3(bbd), full3(cbd), full3(lre), full3(lim), full2(d), full2(w_glu_bf16),
            full2(b_glu),
        ],
        out_specs=pl.BlockSpec((b, tb, width), lambda i: (0, i, 0)),
        out_shape=jax.ShapeDtypeStruct((b, s, width), BF16),
        scratch_shapes=[
            pltpu.VMEM((8, b * tbp, LANES), F32),
            pltpu.VMEM((n_lt, 8, b, LANES), F32),
            pltpu.VMEM((b, tb, width), F32),
        ],
        compiler_params=pltpu.CompilerParams(
            dimension_semantics=("arbitrary",), vmem_limit_bytes=VMEM_LIMIT),
        name="ssm_glu",
    )(u, bbd, cbd, lre, lim, d, w_glu_bf16, b_glu)


def _layernorm(y, g, b):
    mu = jnp.mean(y, axis=-1, keepdims=True)
    yc = y - mu
    var = jnp.mean(yc * yc, axis=-1, keepdims=True)
    return yc * lax.rsqrt(var + LN_EPS) * g + b


N_ROUTE_COLS = N_EXPERT_GROUPS + N_EXPERTS


def _store_token_tiles(ref, val):
    n = val.shape[0]
    for j in range(SUBLANES):
        ref[pl.ds(j, n, stride=SUBLANES), :] = val[:, j * LANES:(j + 1) * LANES]


def _load_token_tiles(ref, n):
    return jnp.concatenate(
        [ref[pl.ds(j, n, stride=SUBLANES), :] for j in range(SUBLANES)], axis=1)


def _out_proj_body(a_ref, z_ref, x_ref, wo_ref, g_ref, b_ref, wr_ref, h8_ref, route_ref, *, alpha):
    aw = a_ref.shape[2]
    mix = jnp.dot(a_ref[0], wo_ref[0:aw, :], preferred_element_type=F32)
    mix += jnp.dot(z_ref[0], wo_ref[aw:, :], preferred_element_type=F32)
    h = _layernorm(alpha * x_ref[0] + mix, g_ref[...], b_ref[...])
    _store_token_tiles(h8_ref, h)

    h_hi = h.astype(BF16)
    h_lo = (h - h_hi.astype(F32)).astype(BF16)
    lg = jnp.dot(h_hi, wr_ref[...], preferred_element_type=F32)
    lg += jnp.dot(h_lo, wr_ref[...], preferred_element_type=F32)
    lg = lg + pltpu.roll(lg, LANES - N_ROUTE_COLS, 1)

    lane = lax.broadcasted_iota(I32, lg.shape, 1)

    def argmax_lane(vals):
        mx = jnp.max(vals, axis=-1, keepdims=True)
        idx = jnp.min(jnp.where(vals == mx, lane, LANES), axis=-1, keepdims=True)
        return mx, idx

    gl = jnp.where(lane < N_EXPERT_GROUPS, lg, NEG_BIG)
    g_max, g_idx = argmax_lane(gl)
    g_p = 1.0 / jnp.sum(jnp.exp(gl - g_max), axis=-1, keepdims=True)

    e_lo = N_EXPERT_GROUPS + g_idx * EXPERTS_PER_GROUP
    el = jnp.where((lane >= e_lo) & (lane < e_lo + EXPERTS_PER_GROUP), lg, NEG_BIG)
    e1, i1 = argmax_lane(el)
    e2, i2 = argmax_lane(jnp.where(lane == i1, NEG_BIG, el))
    r = jnp.exp(e2 - e1)
    w1 = g_p / (1.0 + r)
    w2 = g_p * r / (1.0 + r)
    route = jnp.where(lane == 0, (i1 - N_EXPERT_GROUPS).astype(F32), 0.0)
    route = jnp.where(lane == 1, (i2 - N_EXPERT_GROUPS).astype(F32), route)
    route = jnp.where(lane == 2, w1, route)
    route = jnp.where(lane == 3, w2, route)
    route_ref[0] = route


def _out_proj(a, z, x, w_out_bf16, ln_g, ln_b, w_route, *, alpha, ts):
    b, s, d = x.shape
    aw = a.shape[2]
    half = pl.BlockSpec((1, ts, aw), lambda bi, i: (bi, i, 0))
    row = pl.BlockSpec((1, ts, d), lambda bi, i: (bi, i, 0))
    full2 = lambda arr: pl.BlockSpec(arr.shape, lambda bi, i: (0, 0))
    assert d == SUBLANES * LANES
    n_st = s // ts
    return pl.pallas_call(
        functools.partial(_out_proj_body, alpha=alpha),
        grid=(b, n_st),
        in_specs=[half, half, row, full2(w_out_bf16), full2(ln_g), full2(ln_b), full2(w_route)],
        out_specs=[
            pl.BlockSpec((ts * SUBLANES, LANES), lambda bi, i: (bi * n_st + i, 0)),
            pl.BlockSpec((1, ts, LANES), lambda bi, i: (bi, i, 0)),
        ],
        out_shape=[
            jax.ShapeDtypeStruct((b * s * SUBLANES, LANES), F32),
            jax.ShapeDtypeStruct((b, s, LANES), F32),
        ],
        compiler_params=pltpu.CompilerParams(
            dimension_semantics=("parallel", "parallel"), vmem_limit_bytes=VMEM_LIMIT),
        name="out_proj_router",
    )(a, z, x, w_out_bf16, ln_g, ln_b, w_route)


ISSUE_UNROLL = 8


def _moe_body(be_ref, src_ref, nv_ref, nused_ref, grow_ref, srow_ref, h8_hbm, wg_ref, wu_ref,
              wd_ref, out_hbm, xbuf, ybuf, wgb, wub, wdb, gsem, ssem, *, bm, n_tok):
    i = pl.program_id(0)
    n_used = nused_ref[0]
    n_pairs = 2 * n_tok
    tile = SUBLANES

    def tile_row(r):
        return r * tile if isinstance(r, int) else pl.multiple_of(r * tile, tile)

    def gather_copy(blk, buf_slot, r):
        src_row = grow_ref[jnp.minimum(src_ref[blk] + r, n_pairs - 1)]
        return pltpu.make_async_copy(
            h8_hbm.at[pl.ds(pl.multiple_of(src_row, tile), tile)],
            xbuf.at[buf_slot, pl.ds(tile_row(r), tile)],
            gsem.at[buf_slot])

    def scatter_copy(blk, buf_slot, r):
        blk_c = jnp.maximum(blk, 0)
        n_valid = jnp.where(blk >= 0, nv_ref[blk_c], 0)
        real_row = srow_ref[jnp.minimum(src_ref[blk_c] + r, n_pairs - 1)]
        pad_row = (n_pairs + buf_slot * bm + r) * tile
        dst_row = jnp.where(r < n_valid, real_row, pad_row)
        return pltpu.make_async_copy(
            ybuf.at[buf_slot, pl.ds(tile_row(r), tile)],
            out_hbm.at[pl.ds(pl.multiple_of(dst_row, tile), tile)],
            ssem.at[buf_slot])

    def issue_loop(make_copy, blk, buf_slot):
        def trip(g, c):
            for q in range(ISSUE_UNROLL):
                make_copy(blk, buf_slot, pl.multiple_of(g * ISSUE_UNROLL, ISSUE_UNROLL) + q).start()
            return c
        lax.fori_loop(0, bm // ISSUE_UNROLL, trip, 0)

    def wait_gather(buf_slot):
        pltpu.make_async_copy(h8_hbm.at[pl.ds(0, bm * tile)], xbuf.at[buf_slot],
                              gsem.at[buf_slot]).wait()

    def wait_scatter(buf_slot):
        pltpu.make_async_copy(ybuf.at[buf_slot], out_hbm.at[pl.ds(0, bm * tile)],
                              ssem.at[buf_slot]).wait()

    @pl.when(i == 0)
    def _():
        issue_loop(gather_copy, 0, 0)
        ybuf[...] = jnp.zeros_like(ybuf)
        for s in range(2):
            pad_rows = out_hbm.at[pl.ds((n_pairs + s * bm) * tile, bm * tile)]
            pltpu.make_async_copy(ybuf.at[s], pad_rows, ssem.at[s]).start()
            pltpu.make_async_copy(ybuf.at[s], pad_rows, ssem.at[s]).wait()

    def block_step(slot):
        other = 1 - slot
        wait_gather(slot)

        @pl.when(i >= 1)
        def _():
            wait_scatter(slot)

        prev = be_ref[jnp.maximum(i - 1, 0)]

        @pl.when((i == 0) | (be_ref[i] != prev))
        def _():
            wgb[...] = wg_ref[0].astype(BF16)
            wub[...] = wu_ref[0].astype(BF16)
            wdb[...] = wd_ref[0].astype(BF16)

        x = _load_token_tiles(xbuf.at[slot], bm).astype(BF16)
        nxt = jnp.minimum(i + 1, n_used - 1)
        for r in range(bm):
            gather_copy(nxt, other, r).start()
            scatter_copy(i - 1, other, r).start()
        gate = jnp.dot(x, wgb[...], preferred_element_type=F32)
        up = jnp.dot(x, wub[...], preferred_element_type=F32)
        hid = (jax.nn.silu(gate) * up).astype(BF16)
        y = jnp.dot(hid, wdb[...], preferred_element_type=F32)
        _store_token_tiles(ybuf.at[slot], y)

    for s in range(2):
        pl.when((i < n_used) & (i % 2 == s))(functools.partial(block_step, s))

    @pl.when(i == n_used)
    def _():
        last_slot = (n_used - 1) % 2
        issue_loop(scatter_copy, n_used - 1, last_slot)
        wait_scatter(0)
        wait_scatter(1)
        pltpu.make_async_copy(h8_hbm.at[pl.ds(0, bm * tile)], xbuf.at[1 - last_slot],
                              gsem.at[1 - last_slot]).wait()


def _moe_ffn(block_e, block_src, block_nv, n_used, gather_row, scatter_row, h8, w_gate, w_up,
             w_down, *, bm):
    n_tok = h8.shape[0] // SUBLANES
    d = SUBLANES * LANES
    n_blocks = block_e.shape[0]
    de = w_gate.shape[2]
    out_rows = (2 * n_tok + 2 * bm) * SUBLANES
    wspec = lambda shape: pl.BlockSpec(shape, lambda i, be, *_: (be[i], 0, 0))
    grid_spec = pltpu.PrefetchScalarGridSpec(
        num_scalar_prefetch=6,
        grid=(n_blocks,),
        in_specs=[
            pl.BlockSpec(memory_space=pl.ANY),
            wspec((1, d, de)), wspec((1, d, de)), wspec((1, de, d)),
        ],
        out_specs=pl.BlockSpec(memory_space=pl.ANY),
        scratch_shapes=[
            pltpu.VMEM((2, bm * SUBLANES, LANES), F32),
            pltpu.VMEM((2, bm * SUBLANES, LANES), F32),
            pltpu.VMEM((d, de), BF16),
            pltpu.VMEM((d, de), BF16),
            pltpu.VMEM((de, d), BF16),
            pltpu.SemaphoreType.DMA((2,)),
            pltpu.SemaphoreType.DMA((2,)),
        ],
    )
    return pl.pallas_call(
        functools.partial(_moe_body, bm=bm, n_tok=n_tok),
        grid_spec=grid_spec,
        out_shape=jax.ShapeDtypeStruct((out_rows, LANES), F32),
        compiler_params=pltpu.CompilerParams(
            dimension_semantics=("arbitrary",), vmem_limit_bytes=VMEM_LIMIT),
        name="moe_ffn",
    )(block_e, block_src, block_nv, n_used, gather_row, scatter_row, h8, w_gate, w_up, w_down)


def _combine_body(h8_ref, y0_ref, y1_ref, route_ref, g_ref, b_ref, o_ref, *, tt, alpha):
    route = route_ref[...]
    w0 = route[:, 2:3]
    w1 = route[:, 3:4]
    h = _load_token_tiles(h8_ref, tt)
    moe = w0 * _load_token_tiles(y0_ref, tt) + w1 * _load_token_tiles(y1_ref, tt)
    o_ref[...] = _layernorm(alpha * h + moe, g_ref[...], b_ref[...])


def _combine(h8, out2, route, ln_g, ln_b, *, alpha, tt):
    t = route.shape[0]
    d = SUBLANES * LANES
    n_tiles = t // tt
    tile_rows = pl.BlockSpec((tt * SUBLANES, LANES), lambda i: (i, 0))
    return pl.pallas_call(
        functools.partial(_combine_body, tt=tt, alpha=alpha),
        grid=(n_tiles,),
        in_specs=[
            tile_rows,
            tile_rows,
            pl.BlockSpec((tt * SUBLANES, LANES), lambda i: (n_tiles + i, 0)),
            pl.BlockSpec((tt, LANES), lambda i: (i, 0)),
            pl.BlockSpec((1, d), lambda i: (0, 0)),
            pl.BlockSpec((1, d), lambda i: (0, 0)),
        ],
        out_specs=pl.BlockSpec((tt, d), lambda i: (i, 0)),
        out_shape=jax.ShapeDtypeStruct((t, d), F32),
        compiler_params=pltpu.CompilerParams(
            dimension_semantics=("parallel",), vmem_limit_bytes=VMEM_LIMIT),
        name="moe_combine",
    )(h8, out2, out2, route, ln_g, ln_b)


def _moe_layout(route, *, bm):
    t = route.shape[0]
    tk = 2 * t
    flat_e = route[:, 0:2].astype(I32).reshape(-1)
    key = jnp.sort(flat_e * tk + jnp.arange(tk, dtype=I32))
    order = key % tk
    experts = jnp.arange(N_EXPERTS, dtype=I32)
    counts = jnp.sum((flat_e[:, None] == experts[None, :]).astype(I32), axis=0)
    starts = jnp.cumsum(counts) - counts
    pcounts = (counts + bm - 1) // bm * bm
    pends = jnp.cumsum(pcounts)
    pstarts = pends - pcounts
    n_blocks = (tk + N_EXPERTS * bm) // bm
    row0 = jnp.arange(n_blocks, dtype=I32) * bm
    block_e = jnp.minimum(
        jnp.sum((pends[None, :] <= row0[:, None]).astype(I32), axis=1), N_EXPERTS - 1)
    off = row0 - pstarts[block_e]
    block_src = starts[block_e] + off
    block_nv = jnp.clip(counts[block_e] - off, 0, bm)
    n_used = (pends[-1] // bm).astype(I32).reshape(1)
    gather_row = (order // 2) * SUBLANES
    scatter_row = ((order % 2) * t + order // 2) * SUBLANES
    return (block_e, block_src.astype(I32), block_nv.astype(I32), n_used, gather_row.astype(I32),
            scatter_row.astype(I32))


def _layer(x, w_in, lam_q1, lam_k1, lam_q2, lam_k2, subln_g, a_re, a_im, log_dt, b_re, b_im,
           c_re, c_im, d, w_glu, b_glu, w_out, ln1_g, ln1_b, w_rg, w_re, w_gate, w_up, w_down,
           ln2_g, ln2_b, *, layer_idx, alpha, ts, qb, tb, bm, tt):
    b, s, dm = x.shape
    lam_init = 0.8 - 0.6 * math.exp(-0.3 * layer_idx)

    q, k, v, u = _in_proj(x, w_in.astype(BF16), ts=ts)

    lam_params = jnp.stack([lam_q1, lam_k1, lam_q2, lam_k2]).astype(F32)
    a_out = _diff_attention(q, k, v, lam_params, subln_g.reshape(1, -1).astype(F32),
                            lam_init=lam_init, qb=qb)

    bbd, cbd, lre, lim = _ssm_params(a_re, a_im, log_dt, b_re, b_im, c_re, c_im)
    s_out = _ssm_glu(u, bbd, cbd, lre, lim, d.reshape(1, -1).astype(F32), w_glu.astype(BF16),
                     b_glu.reshape(1, -1).astype(F32), tb=tb)

    w_r = jnp.concatenate([w_rg, w_re], axis=1).astype(F32)
    w_r_hi = w_r.astype(BF16)
    w_r_lo = (w_r - w_r_hi.astype(F32)).astype(BF16)
    w_route = jnp.concatenate(
        [w_r_hi, w_r_lo, jnp.zeros((dm, LANES - 2 * N_ROUTE_COLS), BF16)], axis=1)
    h8, route = _out_proj(a_out, s_out, x, w_out.astype(BF16), ln1_g.reshape(1, -1),
                          ln1_b.reshape(1, -1), w_route, alpha=alpha, ts=ts)

    route = route.reshape(b * s, LANES)
    out2 = _moe_ffn(*_moe_layout(route, bm=bm), h8, w_gate, w_up, w_down, bm=bm)
    out = _combine(h8, out2, route, ln2_g.reshape(1, -1), ln2_b.reshape(1, -1), alpha=alpha, tt=tt)
    return out.reshape(b, s, dm)


def kernel(x, w_in, lam_q1, lam_k1, lam_q2, lam_k2, subln_g, ssm_a_re, ssm_a_im, ssm_log_dt, ssm_b_re, ssm_b_im, ssm_c_re, ssm_c_im, ssm_d, w_glu, b_glu, w_out, ln1_g, ln1_b, w_router_group, w_router_expert, w_exp_gate, w_exp_up, w_exp_down, ln2_g, ln2_b):
    depth = w_in.shape[0]
    alpha = (2.0 * depth) ** 0.25
    for i in range(depth):
        x = _layer(
            x, w_in[i], lam_q1[i], lam_k1[i], lam_q2[i], lam_k2[i], subln_g[i], ssm_a_re[i],
            ssm_a_im[i], ssm_log_dt[i], ssm_b_re[i], ssm_b_im[i], ssm_c_re[i], ssm_c_im[i],
            ssm_d[i], w_glu[i], b_glu[i], w_out[i], ln1_g[i], ln1_b[i], w_router_group[i],
            w_router_expert[i], w_exp_gate[i], w_exp_up[i], w_exp_down[i], ln2_g[i], ln2_b[i],
            layer_idx=i, alpha=alpha, ts=512, qb=256, tb=256, bm=256, tt=256)
    return x
```

```python
import functools
import math

import jax
import jax.numpy as jnp
from jax import lax
from jax.experimental import pallas as pl
from jax.experimental.pallas import tpu as pltpu

F32 = jnp.float32
BF16 = jnp.bfloat16
I32 = jnp.int32

DEPTH = 1
N_HEADS = 4
HEAD_DIM = 64
ATTN_WIDTH = N_HEADS * 2 * HEAD_DIM
CHUNK = 64
GROUP_CH = 16
SSM_STATE = 64
N_EXPERT_GROUPS = 4
EXPERTS_PER_GROUP = 8
N_EXPERTS = N_EXPERT_GROUPS * EXPERTS_PER_GROUP
LN_EPS = 1e-5
RMS_EPS = 1e-5

LANES = 128
SUBLANES = 8
VMEM_LIMIT = 56 * 1024 * 1024

NEG_BIG = -1e30


def _in_proj_body(x_ref, w_ref, q_ref, k_ref, v_ref, u_ref, *, width, q_scale):
    x = x_ref[0].astype(BF16)

    def proj(c):
        return jnp.dot(x, w_ref[:, c * width:(c + 1) * width], preferred_element_type=F32)

    q_ref[0] = (proj(0) * q_scale).astype(BF16)
    k_ref[0] = proj(1).astype(BF16)
    v_ref[0] = proj(2).astype(BF16)
    u_ref[0] = proj(3)


def _in_proj(x, w_in_bf16, *, ts):
    b, s, d = x.shape
    width = ATTN_WIDTH
    out_block = pl.BlockSpec((1, ts, width), lambda bi, i: (bi, i, 0))
    return pl.pallas_call(
        functools.partial(_in_proj_body, width=width, q_scale=HEAD_DIM ** -0.5 * math.log2(math.e)),
        grid=(b, s // ts),
        in_specs=[
            pl.BlockSpec((1, ts, d), lambda bi, i: (bi, i, 0)),
            pl.BlockSpec((d, 4 * width), lambda bi, i: (0, 0)),
        ],
        out_specs=[out_block, out_block, out_block, out_block],
        out_shape=[
            jax.ShapeDtypeStruct((b, s, width), BF16),
            jax.ShapeDtypeStruct((b, s, width), BF16),
            jax.ShapeDtypeStruct((b, s, width), BF16),
            jax.ShapeDtypeStruct((b, s, width), F32),
        ],
        compiler_params=pltpu.CompilerParams(
            dimension_semantics=("parallel", "parallel"), vmem_limit_bytes=VMEM_LIMIT),
        name="in_proj",
    )(x, w_in_bf16)


ATTN_ROWS = 32


def _attn_body(lamp_ref, g_ref, q_ref, k_ref, v_ref, o_ref,
               qq_scr, m_scr, l_scr, acc_scr, *, lam_init, n_qt, qb, kb):
    lp = lamp_ref[...]
    s1 = jnp.sum(lp[0:1] * lp[1:2], axis=-1, keepdims=True)
    s2 = jnp.sum(lp[2:3] * lp[3:4], axis=-1, keepdims=True)
    lam = jnp.exp(s1) - jnp.exp(s2) + lam_init

    lane = lax.broadcasted_iota(I32, (qb, 2 * HEAD_DIM), 1)
    n_sub = 2 * qb // ATTN_ROWS

    def kv_step(start, width, diag_tile=False):
        start = pl.multiple_of(start, qb)
        kblk = k_ref[0, pl.ds(start, width), :]
        s = lax.dot_general(
            qq_scr[...], kblk, (((1,), (1,)), ((), ())), preferred_element_type=F32)
        if diag_tile:
            q_row = lax.broadcasted_iota(I32, (2 * qb, qb), 0) % qb
            k_col = lax.broadcasted_iota(I32, (2 * qb, qb), 1)
            diag = jnp.where(k_col // CHUNK <= q_row // CHUNK, s[:, width - qb:], NEG_BIG)
            s = diag if width == qb else jnp.concatenate([s[:, :width - qb], diag], axis=1)
        m_old = m_scr[...]
        m_new = jnp.maximum(m_old, jnp.max(s, axis=-1, keepdims=True))
        a = jnp.exp2(m_old - m_new)
        m_scr[...] = m_new
        l_parts, p_parts = [], []
        for c in range(n_sub):
            rows = slice(c * ATTN_ROWS, (c + 1) * ATTN_ROWS)
            m_c = m_new[rows]
            cols = [jnp.exp2(s[rows, j * LANES:(j + 1) * LANES] - m_c) for j in range(width // LANES)]
            l_parts.append(functools.reduce(lambda x, y: x + y, cols))
            p_parts.append(jnp.concatenate(cols, axis=1).astype(BF16))
        l_scr[...] = a * l_scr[...] + jnp.concatenate(l_parts, axis=0)
        vblk = v_ref[0, pl.ds(start, width), :]
        pv = jnp.dot(jnp.concatenate(p_parts, axis=0), vblk, preferred_element_type=F32)
        acc_scr[...] = a * acc_scr[...] + pv

    def q_tile(i, carry_unused):
        q = q_ref[0, pl.ds(pl.multiple_of(i * qb, qb), qb), :]
        zero = jnp.zeros_like(q)
        qq_scr[0:qb, :] = jnp.where(lane < HEAD_DIM, q, zero)
        qq_scr[qb:2 * qb, :] = jnp.where(lane >= HEAD_DIM, q, zero)
        m_scr[...] = jnp.full(m_scr.shape, NEG_BIG, F32)
        l_scr[...] = jnp.zeros(l_scr.shape, F32)
        acc_scr[...] = jnp.zeros(acc_scr.shape, F32)

        tiles_per_block = kb // qb

        def full_block(j, c):
            kv_step(j * kb, kb)
            return c

        lax.fori_loop(0, i // tiles_per_block, full_block, 0)

        for r in range(tiles_per_block):
            @pl.when(i % tiles_per_block == r)
            def _(r=r):
                kv_step((i - r) * qb, (r + 1) * qb, diag_tile=True)

        o = acc_scr[...] / jnp.sum(l_scr[...], axis=-1, keepdims=True)
        od = o[:qb] - lam * o[qb:]
        ms = jnp.mean(od * od, axis=-1, keepdims=True)
        od = od * lax.rsqrt(ms + RMS_EPS) * g_ref[...] * (1.0 - lam_init)
        o_ref[0, pl.ds(pl.multiple_of(i * qb, qb), qb), :] = od.astype(BF16)
        return carry_unused

    lax.fori_loop(0, n_qt, q_tile, 0)


def _diff_attention(q, k, v, lam_params, subln_g, *, lam_init, qb):
    b, s, width = q.shape
    hw = 2 * HEAD_DIM
    kb = 4 * qb
    seq_block = pl.BlockSpec((1, s, hw), lambda bi, h: (bi, 0, h))
    return pl.pallas_call(
        functools.partial(_attn_body, lam_init=lam_init, n_qt=s // qb, qb=qb, kb=kb),
        grid=(b, N_HEADS),
        in_specs=[
            pl.BlockSpec((4, HEAD_DIM), lambda bi, h: (0, 0)),
            pl.BlockSpec((1, hw), lambda bi, h: (0, 0)),
            seq_block, seq_block, seq_block,
        ],
        out_specs=seq_block,
        out_shape=jax.ShapeDtypeStruct((b, s, width), BF16),
        scratch_shapes=[
            pltpu.VMEM((2 * qb, hw), BF16),
            pltpu.VMEM((2 * qb, LANES), F32),
            pltpu.VMEM((2 * qb, LANES), F32),
            pltpu.VMEM((2 * qb, hw), F32),
        ],
        compiler_params=pltpu.CompilerParams(
            dimension_semantics=("parallel", "parallel"), vmem_limit_bytes=VMEM_LIMIT),
        name="diff_attn",
    )(lam_params, subln_g, q, k, v)


SCAN_UNROLL = 4


def _ssm_body(u_ref, bbd_ref, cbd_ref, lre_ref, lim_ref, d_ref, wg_ref, bg_ref, z_ref,
              scr, hst, ysc, *, tb, tbp, n_lt):
    nb = u_ref.shape[0]
    n_half = 4

    @pl.when(pl.program_id(0) == 0)
    def _():
        hst[...] = jnp.zeros_like(hst)

    for lt in range(n_lt):
        lanes = slice(lt * LANES, (lt + 1) * LANES)
        for b in range(nb):
            ub = u_ref[b, :, lanes].astype(BF16)
            bu = jnp.dot(ub, bbd_ref[lt], preferred_element_type=F32)
            for k in range(2 * n_half):
                scr[k, b * tbp:b * tbp + tb, :] = bu[:, k * LANES:(k + 1) * LANES]

        lr = [jnp.broadcast_to(lre_ref[lt, :, k * LANES:(k + 1) * LANES], (nb, LANES))
              for k in range(n_half)]
        li = [jnp.broadcast_to(lim_ref[lt, :, k * LANES:(k + 1) * LANES], (nb, LANES))
              for k in range(n_half)]

        def step(t, h):
            new_re, new_im = [], []
            for k in range(n_half):
                rows = pl.ds(t, nb, stride=tbp)
                hr, hi = h[k], h[n_half + k]
                nr = lr[k] * hr - li[k] * hi + scr[k, rows, :]
                ni = lr[k] * hi + li[k] * hr + scr[n_half + k, rows, :]
                scr[k, rows, :] = nr
                scr[n_half + k, rows, :] = ni
                new_re.append(nr)
                new_im.append(ni)
            return tuple(new_re + new_im)

        h = lax.fori_loop(0, tb, step, tuple(hst[lt, k] for k in range(2 * n_half)),
                          unroll=SCAN_UNROLL)
        for k in range(2 * n_half):
            hst[lt, k] = h[k]

        for b in range(nb):
            hb = jnp.concatenate(
                [scr[k, b * tbp:b * tbp + tb, :] for k in range(2 * n_half)], axis=1)
            yb = jnp.dot(hb.astype(BF16), cbd_ref[lt], preferred_element_type=F32)
            ysc[b, :, lanes] = yb + d_ref[:, lanes] * u_ref[b, :, lanes]

    for b in range(nb):
        z = jax.nn.gelu(ysc[b])
        gate = jax.nn.sigmoid(
            jnp.dot(z.astype(BF16), wg_ref[...], preferred_element_type=F32) + bg_ref[...])
        z_ref[b] = (z * gate).astype(BF16)


def _ssm_params(a_re, a_im, log_dt, b_re, b_im, c_re, c_im):
    g = a_re.shape[0]
    n_lt = g * GROUP_CH // LANES
    gpt = g // n_lt
    dt = jnp.exp(log_dt.astype(F32))[:, None]
    ar, ai = a_re.astype(F32), a_im.astype(F32)
    mag = jnp.exp(ar * dt)
    lam_re = mag * jnp.cos(ai * dt)
    lam_im = mag * jnp.sin(ai * dt)
    nr, ni = lam_re - 1.0, lam_im
    den = ar * ar + ai * ai
    cr = (nr * ar + ni * ai) / den
    ci = (ni * ar - nr * ai) / den
    bb_re = cr[..., None] * b_re - ci[..., None] * b_im
    bb_im = cr[..., None] * b_im + ci[..., None] * b_re
    eye = jnp.eye(gpt, dtype=F32)

    def pack_b(bb):
        bb = bb.reshape(n_lt, gpt, SSM_STATE, GROUP_CH)
        m = jnp.einsum('lgpc,gh->lgchp', bb, eye)
        return m.reshape(n_lt, gpt * GROUP_CH, gpt * SSM_STATE)

    def pack_c(cc):
        cc = cc.astype(F32).reshape(n_lt, gpt, GROUP_CH, SSM_STATE)
        m = jnp.einsum('lgcp,gh->lgphc', cc, eye)
        return m.reshape(n_lt, gpt * SSM_STATE, gpt * GROUP_CH)

    bbd = jnp.concatenate([pack_b(bb_re), pack_b(bb_im)], axis=2).astype(BF16)
    cbd = jnp.concatenate([pack_c(c_re), -pack_c(c_im)], axis=1).astype(BF16)
    lre = lam_re.reshape(n_lt, 1, gpt * SSM_STATE)
    lim = lam_im.reshape(n_lt, 1, gpt * SSM_STATE)
    return bbd, cbd, lre, lim


def _ssm_glu(u, bbd, cbd, lre, lim, d, w_glu_bf16, b_glu, *, tb):
    b, s, width = u.shape
    assert b == SUBLANES
    n_lt = width // LANES
    tbp = tb + SUBLANES
    full3 = lambda a: pl.BlockSpec(a.shape, lambda i: (0, 0, 0))
    full2 = lambda a: pl.BlockSpec(a.shape, lambda i: (0, 0))
    return pl.pallas_call(
        functools.partial(_ssm_body, tb=tb, tbp=tbp, n_lt=n_lt),
        grid=(s // tb,),
        in_specs=[
            pl.BlockSpec((b, tb, width), lambda i: (0, i, 0)),
            full3(bbd), full3(cbd), full3(lre), full3(lim), full2(d), full2(w_glu_bf16),
            full2(b_glu),
        ],
        out_specs=pl.BlockSpec((b, tb, width), lambda i: (0, i, 0)),
        out_shape=jax.ShapeDtypeStruct((b, s, width), BF16),
        scratch_shapes=[
            pltpu.VMEM((8, b * tbp, LANES), F32),
            pltpu.VMEM((n_lt, 8, b, LANES), F32),
            pltpu.VMEM((b, tb, width), F32),
        ],
        compiler_params=pltpu.CompilerParams(
            dimension_semantics=("arbitrary",), vmem_limit_bytes=VMEM_LIMIT),
        name="ssm_glu",
    )(u, bbd, cbd, lre, lim, d, w_glu_bf16, b_glu)


def _layernorm(y, g, b):
    mu = jnp.mean(y, axis=-1, keepdims=True)
    yc = y - mu
    var = jnp.mean(yc * yc, axis=-1, keepdims=True)
    return yc * lax.rsqrt(var + LN_EPS) * g + b


N_ROUTE_COLS = N_EXPERT_GROUPS + N_EXPERTS


def _store_token_tiles(ref, val):
    n = val.shape[0]
    for j in range(SUBLANES):
        ref[pl.ds(j, n, stride=SUBLANES), :] = val[:, j * LANES:(j + 1) * LANES]


def _load_token_tiles(ref, n):
    return jnp.concatenate(
        [ref[pl.ds(j, n, stride=SUBLANES), :] for j in range(SUBLANES)], axis=1)


OUT_PROJ_SPLITS = 2


def _out_proj_body(a_ref, z_ref, x_ref, wo_ref, g_ref, b_ref, wr_ref, h8_ref, route_ref, *, alpha):
    n = a_ref.shape[1] // OUT_PROJ_SPLITS
    for part in range(OUT_PROJ_SPLITS):
        rows = pl.ds(part * n, n)
        _out_proj_rows(a_ref[0, rows, :], z_ref[0, rows, :], x_ref[0, rows, :], wo_ref, g_ref,
                       b_ref, wr_ref, h8_ref.at[pl.ds(part * n * SUBLANES, n * SUBLANES)],
                       route_ref.at[0, rows], alpha=alpha)


def _out_proj_rows(a, z, x, wo_ref, g_ref, b_ref, wr_ref, h8_ref, route_ref, *, alpha):
    aw = a.shape[1]
    mix = jnp.dot(a, wo_ref[0:aw, :], preferred_element_type=F32)
    mix += jnp.dot(z, wo_ref[aw:, :], preferred_element_type=F32)
    h = _layernorm(alpha * x + mix, g_ref[...], b_ref[...])
    _store_token_tiles(h8_ref, h)

    h_hi = h.astype(BF16)
    h_lo = (h - h_hi.astype(F32)).astype(BF16)
    lg = jnp.dot(h_hi, wr_ref[...], preferred_element_type=F32)
    lg += jnp.dot(h_lo, wr_ref[...], preferred_element_type=F32)
    lg = lg + pltpu.roll(lg, LANES - N_ROUTE_COLS, 1)

    lane = lax.broadcasted_iota(I32, lg.shape, 1)

    def argmax_lane(vals):
        mx = jnp.max(vals, axis=-1, keepdims=True)
        idx = jnp.min(jnp.where(vals == mx, lane, LANES), axis=-1, keepdims=True)
        return mx, idx

    gl = jnp.where(lane < N_EXPERT_GROUPS, lg, NEG_BIG)
    g_max, g_idx = argmax_lane(gl)
    g_p = 1.0 / jnp.sum(jnp.exp(gl - g_max), axis=-1, keepdims=True)

    e_lo = N_EXPERT_GROUPS + g_idx * EXPERTS_PER_GROUP
    el = jnp.where((lane >= e_lo) & (lane < e_lo + EXPERTS_PER_GROUP), lg, NEG_BIG)
    e1, i1 = argmax_lane(el)
    e2, i2 = argmax_lane(jnp.where(lane == i1, NEG_BIG, el))
    r = jnp.exp(e2 - e1)
    w1 = g_p / (1.0 + r)
    w2 = g_p * r / (1.0 + r)
    route = jnp.where(lane == 0, (i1 - N_EXPERT_GROUPS).astype(F32), 0.0)
    route = jnp.where(lane == 1, (i2 - N_EXPERT_GROUPS).astype(F32), route)
    route = jnp.where(lane == 2, w1, route)
    route = jnp.where(lane == 3, w2, route)
    route_ref[...] = route


def _out_proj(a, z, x, w_out_bf16, ln_g, ln_b, w_route, *, alpha, ts):
    b, s, d = x.shape
    aw = a.shape[2]
    half = pl.BlockSpec((1, ts, aw), lambda bi, i: (bi, i, 0))
    row = pl.BlockSpec((1, ts, d), lambda bi, i: (bi, i, 0))
    full2 = lambda arr: pl.BlockSpec(arr.shape, lambda bi, i: (0, 0))
    assert d == SUBLANES * LANES
    n_st = s // ts
    return pl.pallas_call(
        functools.partial(_out_proj_body, alpha=alpha),
        grid=(b, n_st),
        in_specs=[half, half, row, full2(w_out_bf16), full2(ln_g), full2(ln_b), full2(w_route)],
        out_specs=[
            pl.BlockSpec((ts * SUBLANES, LANES), lambda bi, i: (bi * n_st + i, 0)),
            pl.BlockSpec((1, ts, LANES), lambda bi, i: (bi, i, 0)),
        ],
        out_shape=[
            jax.ShapeDtypeStruct((b * s * SUBLANES, LANES), F32),
            jax.ShapeDtypeStruct((b, s, LANES), F32),
        ],
        compiler_params=pltpu.CompilerParams(
            dimension_semantics=("parallel", "parallel"), vmem_limit_bytes=VMEM_LIMIT),
        name="out_proj_router",
    )(a, z, x, w_out_bf16, ln_g, ln_b, w_route)


ISSUE_UNROLL = 8


def _moe_body(be_ref, src_ref, nv_ref, nused_ref, grow_ref, srow_ref, h8_hbm, wg_ref, wu_ref,
              wd_ref, out_hbm, xbuf, ybuf, wgb, wub, wdb, gsem, ssem, *, bm, n_tok):
    i = pl.program_id(0)
    n_used = nused_ref[0]
    n_pairs = 2 * n_tok
    tile = SUBLANES

    def tile_row(r):
        return r * tile if isinstance(r, int) else pl.multiple_of(r * tile, tile)

    def gather_copy(blk, buf_slot, r):
        src_row = grow_ref[jnp.minimum(src_ref[blk] + r, n_pairs - 1)]
        return pltpu.make_async_copy(
            h8_hbm.at[pl.ds(pl.multiple_of(src_row, tile), tile)],
            xbuf.at[buf_slot, pl.ds(tile_row(r), tile)],
            gsem.at[buf_slot])

    def scatter_copy(blk, buf_slot, r):
        blk_c = jnp.maximum(blk, 0)
        n_valid = jnp.where(blk >= 0, nv_ref[blk_c], 0)
        real_row = srow_ref[jnp.minimum(src_ref[blk_c] + r, n_pairs - 1)]
        pad_row = (n_pairs + buf_slot * bm + r) * tile
        dst_row = jnp.where(r < n_valid, real_row, pad_row)
        return pltpu.make_async_copy(
            ybuf.at[buf_slot, pl.ds(tile_row(r), tile)],
            out_hbm.at[pl.ds(pl.multiple_of(dst_row, tile), tile)],
            ssem.at[buf_slot])

    def issue_loop(make_copy, blk, buf_slot):
        def trip(g, c):
            for q in range(ISSUE_UNROLL):
                make_copy(blk, buf_slot, pl.multiple_of(g * ISSUE_UNROLL, ISSUE_UNROLL) + q).start()
            return c
        lax.fori_loop(0, bm // ISSUE_UNROLL, trip, 0)

    def wait_gather(buf_slot):
        pltpu.make_async_copy(h8_hbm.at[pl.ds(0, bm * tile)], xbuf.at[buf_slot],
                              gsem.at[buf_slot]).wait()

    def wait_scatter(buf_slot):
        pltpu.make_async_copy(ybuf.at[buf_slot], out_hbm.at[pl.ds(0, bm * tile)],
                              ssem.at[buf_slot]).wait()

    @pl.when(i == 0)
    def _():
        issue_loop(gather_copy, 0, 0)
        ybuf[...] = jnp.zeros_like(ybuf)
        for s in range(2):
            pad_rows = out_hbm.at[pl.ds((n_pairs + s * bm) * tile, bm * tile)]
            pltpu.make_async_copy(ybuf.at[s], pad_rows, ssem.at[s]).start()
            pltpu.make_async_copy(ybuf.at[s], pad_rows, ssem.at[s]).wait()

    def block_step(slot):
        other = 1 - slot
        wait_gather(slot)

        @pl.when(i >= 1)
        def _():
            wait_scatter(slot)

        prev = be_ref[jnp.maximum(i - 1, 0)]

        @pl.when((i == 0) | (be_ref[i] != prev))
        def _():
            wgb[...] = wg_ref[0].astype(BF16)
            wub[...] = wu_ref[0].astype(BF16)
            wdb[...] = wd_ref[0].astype(BF16)

        x = _load_token_tiles(xbuf.at[slot], bm).astype(BF16)
        nxt = jnp.minimum(i + 1, n_used - 1)
        for r in range(bm):
            gather_copy(nxt, other, r).start()
            scatter_copy(i - 1, other, r).start()
        gate = jnp.dot(x, wgb[...], preferred_element_type=F32)
        up = jnp.dot(x, wub[...], preferred_element_type=F32)
        hid = (jax.nn.silu(gate) * up).astype(BF16)
        y = jnp.dot(hid, wdb[...], preferred_element_type=F32)
        _store_token_tiles(ybuf.at[slot], y)

    for s in range(2):
        pl.when((i < n_used) & (i % 2 == s))(functools.partial(block_step, s))

    @pl.when(i == n_used)
    def _():
        last_slot = (n_used - 1) % 2
        issue_loop(scatter_copy, n_used - 1, last_slot)
        wait_scatter(0)
        wait_scatter(1)
        pltpu.make_async_copy(h8_hbm.at[pl.ds(0, bm * tile)], xbuf.at[1 - last_slot],
                              gsem.at[1 - last_slot]).wait()


def _moe_ffn(block_e, block_src, block_nv, n_used, gather_row, scatter_row, h8, w_gate, w_up,
             w_down, *, bm):
    n_tok = h8.shape[0] // SUBLANES
    d = SUBLANES * LANES
    n_blocks = block_e.shape[0]
    de = w_gate.shape[2]
    out_rows = (2 * n_tok + 2 * bm) * SUBLANES
    wspec = lambda shape: pl.BlockSpec(shape, lambda i, be, *_: (be[i], 0, 0))
    grid_spec = pltpu.PrefetchScalarGridSpec(
        num_scalar_prefetch=6,
        grid=(n_blocks,),
        in_specs=[
            pl.BlockSpec(memory_space=pl.ANY),
            wspec((1, d, de)), wspec((1, d, de)), wspec((1, de, d)),
        ],
        out_specs=pl.BlockSpec(memory_space=pl.ANY),
        scratch_shapes=[
            pltpu.VMEM((2, bm * SUBLANES, LANES), F32),
            pltpu.VMEM((2, bm * SUBLANES, LANES), F32),
            pltpu.VMEM((d, de), BF16),
            pltpu.VMEM((d, de), BF16),
            pltpu.VMEM((de, d), BF16),
            pltpu.SemaphoreType.DMA((2,)),
            pltpu.SemaphoreType.DMA((2,)),
        ],
    )
    return pl.pallas_call(
        functools.partial(_moe_body, bm=bm, n_tok=n_tok),
        grid_spec=grid_spec,
        out_shape=jax.ShapeDtypeStruct((out_rows, LANES), F32),
        compiler_params=pltpu.CompilerParams(
            dimension_semantics=("arbitrary",), vmem_limit_bytes=VMEM_LIMIT),
        name="moe_ffn",
    )(block_e, block_src, block_nv, n_used, gather_row, scatter_row, h8, w_gate, w_up, w_down)


def _combine_body(h8_ref, y0_ref, y1_ref, route_ref, g_ref, b_ref, o_ref, *, tt, alpha):
    route = route_ref[...]
    w0 = route[:, 2:3]
    w1 = route[:, 3:4]
    h = _load_token_tiles(h8_ref, tt)
    moe = w0 * _load_token_tiles(y0_ref, tt) + w1 * _load_token_tiles(y1_ref, tt)
    o_ref[...] = _layernorm(alpha * h + moe, g_ref[...], b_ref[...])


def _combine(h8, out2, route, ln_g, ln_b, *, alpha, tt):
    t = route.shape[0]
    d = SUBLANES * LANES
    n_tiles = t // tt
    tile_rows = pl.BlockSpec((tt * SUBLANES, LANES), lambda i: (i, 0))
    return pl.pallas_call(
        functools.partial(_combine_body, tt=tt, alpha=alpha),
        grid=(n_tiles,),
        in_specs=[
            tile_rows,
            tile_rows,
            pl.BlockSpec((tt * SUBLANES, LANES), lambda i: (n_tiles + i, 0)),
            pl.BlockSpec((tt, LANES), lambda i: (i, 0)),
            pl.BlockSpec((1, d), lambda i: (0, 0)),
            pl.BlockSpec((1, d), lambda i: (0, 0)),
        ],
        out_specs=pl.BlockSpec((tt, d), lambda i: (i, 0)),
        out_shape=jax.ShapeDtypeStruct((t, d), F32),
        compiler_params=pltpu.CompilerParams(
            dimension_semantics=("parallel",), vmem_limit_bytes=VMEM_LIMIT),
        name="moe_combine",
    )(h8, out2, out2, route, ln_g, ln_b)


def _moe_layout(route, *, bm):
    t = route.shape[0]
    tk = 2 * t
    flat_e = route[:, 0:2].astype(I32).reshape(-1)
    key = jnp.sort(flat_e * tk + jnp.arange(tk, dtype=I32))
    order = key % tk
    experts = jnp.arange(N_EXPERTS, dtype=I32)
    counts = jnp.sum((flat_e[:, None] == experts[None, :]).astype(I32), axis=0)
    starts = jnp.cumsum(counts) - counts
    pcounts = (counts + bm - 1) // bm * bm
    pends = jnp.cumsum(pcounts)
    pstarts = pends - pcounts
    n_blocks = (tk + N_EXPERTS * bm) // bm
    row0 = jnp.arange(n_blocks, dtype=I32) * bm
    block_e = jnp.minimum(
        jnp.sum((pends[None, :] <= row0[:, None]).astype(I32), axis=1), N_EXPERTS - 1)
    off = row0 - pstarts[block_e]
    block_src = starts[block_e] + off
    block_nv = jnp.clip(counts[block_e] - off, 0, bm)
    n_used = (pends[-1] // bm).astype(I32).reshape(1)
    gather_row = (order // 2) * SUBLANES
    scatter_row = ((order % 2) * t + order // 2) * SUBLANES
    return (block_e, block_src.astype(I32), block_nv.astype(I32), n_used, gather_row.astype(I32),
            scatter_row.astype(I32))


def _layer(x, w_in, lam_q1, lam_k1, lam_q2, lam_k2, subln_g, a_re, a_im, log_dt, b_re, b_im,
           c_re, c_im, d, w_glu, b_glu, w_out, ln1_g, ln1_b, w_rg, w_re, w_gate, w_up, w_down,
           ln2_g, ln2_b, *, layer_idx, alpha, ts, qb, tb, bm, tt):
    b, s, dm = x.shape
    lam_init = 0.8 - 0.6 * math.exp(-0.3 * layer_idx)

    q, k, v, u = _in_proj(x, w_in.astype(BF16), ts=ts)

    lam_params = jnp.stack([lam_q1, lam_k1, lam_q2, lam_k2]).astype(F32)
    a_out = _diff_attention(q, k, v, lam_params, subln_g.reshape(1, -1).astype(F32),
                            lam_init=lam_init, qb=qb)

    bbd, cbd, lre, lim = _ssm_params(a_re, a_im, log_dt, b_re, b_im, c_re, c_im)
    s_out = _ssm_glu(u, bbd, cbd, lre, lim, d.reshape(1, -1).astype(F32), w_glu.astype(BF16),
                     b_glu.reshape(1, -1).astype(F32), tb=tb)

    w_r = jnp.concatenate([w_rg, w_re], axis=1).astype(F32)
    w_r_hi = w_r.astype(BF16)
    w_r_lo = (w_r - w_r_hi.astype(F32)).astype(BF16)
    w_route = jnp.concatenate(
        [w_r_hi, w_r_lo, jnp.zeros((dm, LANES - 2 * N_ROUTE_COLS), BF16)], axis=1)
    h8, route = _out_proj(a_out, s_out, x, w_out.astype(BF16), ln1_g.reshape(1, -1),
                          ln1_b.reshape(1, -1), w_route, alpha=alpha, ts=ts)

    route = route.reshape(b * s, LANES)
    out2 = _moe_ffn(*_moe_layout(route, bm=bm), h8, w_gate, w_up, w_down, bm=bm)
    out = _combine(h8, out2, route, ln2_g.reshape(1, -1), ln2_b.reshape(1, -1), alpha=alpha, tt=tt)
    return out.reshape(b, s, dm)


def kernel(x, w_in, lam_q1, lam_k1, lam_q2, lam_k2, subln_g, ssm_a_re, ssm_a_im, ssm_log_dt, ssm_b_re, ssm_b_im, ssm_c_re, ssm_c_im, ssm_d, w_glu, b_glu, w_out, ln1_g, ln1_b, w_router_group, w_router_expert, w_exp_gate, w_exp_up, w_exp_down, ln2_g, ln2_b):
    depth = w_in.shape[0]
    alpha = (2.0 * depth) ** 0.25
    for i in range(depth):
        x = _layer(
            x, w_in[i], lam_q1[i], lam_k1[i], lam_q2[i], lam_k2[i], subln_g[i], ssm_a_re[i],
            ssm_a_im[i], ssm_log_dt[i], ssm_b_re[i], ssm_b_im[i], ssm_c_re[i], ssm_c_im[i],
            ssm_d[i], w_glu[i], b_glu[i], w_out[i], ln1_g[i], ln1_b[i], w_router_group[i],
            w_router_expert[i], w_exp_gate[i], w_exp_up[i], w_exp_down[i], ln2_g[i], ln2_b[i],
            layer_idx=i, alpha=alpha, ts=512, qb=256, tb=256, bm=256, tt=256)
    return x
```

```python
import functools
import math

import jax
import jax.numpy as jnp
from jax import lax
from jax.experimental import pallas as pl
from jax.experimental.pallas import tpu as pltpu

F32 = jnp.float32
BF16 = jnp.bfloat16
I32 = jnp.int32

DEPTH = 1
N_HEADS = 4
HEAD_DIM = 64
ATTN_WIDTH = N_HEADS * 2 * HEAD_DIM
CHUNK = 64
GROUP_CH = 16
SSM_STATE = 64
N_EXPERT_GROUPS = 4
EXPERTS_PER_GROUP = 8
N_EXPERTS = N_EXPERT_GROUPS * EXPERTS_PER_GROUP
LN_EPS = 1e-5
RMS_EPS = 1e-5

LANES = 128
SUBLANES = 8
VMEM_LIMIT = 56 * 1024 * 1024

NEG_BIG = -1e30


def _in_proj_body(x_ref, w_ref, q_ref, k_ref, v_ref, u_ref, *, width, q_scale):
    x = x_ref[0].astype(BF16)

    def proj(c):
        return jnp.dot(x, w_ref[:, c * width:(c + 1) * width], preferred_element_type=F32)

    q_ref[0] = (proj(0) * q_scale).astype(BF16)
    k_ref[0] = proj(1).astype(BF16)
    v_ref[0] = proj(2).astype(BF16)
    u_ref[0] = proj(3)


def _in_proj(x, w_in_bf16, *, ts):
    b, s, d = x.shape
    width = ATTN_WIDTH
    out_block = pl.BlockSpec((1, ts, width), lambda bi, i: (bi, i, 0))
    return pl.pallas_call(
        functools.partial(_in_proj_body, width=width, q_scale=HEAD_DIM ** -0.5 * math.log2(math.e)),
        grid=(b, s // ts),
        in_specs=[
            pl.BlockSpec((1, ts, d), lambda bi, i: (bi, i, 0)),
            pl.BlockSpec((d, 4 * width), lambda bi, i: (0, 0)),
        ],
        out_specs=[out_block, out_block, out_block, out_block],
        out_shape=[
            jax.ShapeDtypeStruct((b, s, width), BF16),
            jax.ShapeDtypeStruct((b, s, width), BF16),
            jax.ShapeDtypeStruct((b, s, width), BF16),
            jax.ShapeDtypeStruct((b, s, width), F32),
        ],
        compiler_params=pltpu.CompilerParams(
            dimension_semantics=("parallel", "parallel"), vmem_limit_bytes=VMEM_LIMIT),
        name="in_proj",
    )(x, w_in_bf16)


ATTN_ROWS = 32


def _attn_body(lamp_ref, g_ref, q_ref, k_ref, v_ref, o_ref,
               qq_scr, m_scr, l_scr, acc_scr, *, lam_init, n_qt, qb, kb):
    lp = lamp_ref[...]
    s1 = jnp.sum(lp[0:1] * lp[1:2], axis=-1, keepdims=True)
    s2 = jnp.sum(lp[2:3] * lp[3:4], axis=-1, keepdims=True)
    lam = jnp.exp(s1) - jnp.exp(s2) + lam_init

    lane = lax.broadcasted_iota(I32, (qb, 2 * HEAD_DIM), 1)
    n_sub = 2 * qb // ATTN_ROWS

    def kv_step(start, width, diag_tile=False):
        start = pl.multiple_of(start, qb)
        kblk = k_ref[0, pl.ds(start, width), :]
        s = lax.dot_general(
            qq_scr[...], kblk, (((1,), (1,)), ((), ())), preferred_element_type=F32)
        if diag_tile:
            q_row = lax.broadcasted_iota(I32, (2 * qb, qb), 0) % qb
            k_col = lax.broadcasted_iota(I32, (2 * qb, qb), 1)
            diag = jnp.where(k_col // CHUNK <= q_row // CHUNK, s[:, width - qb:], NEG_BIG)
            s = diag if width == qb else jnp.concatenate([s[:, :width - qb], diag], axis=1)
        m_old = m_scr[...]
        m_new = jnp.maximum(m_old, jnp.max(s, axis=-1, keepdims=True))
        a = jnp.exp2(m_old - m_new)
        m_scr[...] = m_new
        l_parts, p_parts = [], []
        for c in range(n_sub):
            rows = slice(c * ATTN_ROWS, (c + 1) * ATTN_ROWS)
            m_c = m_new[rows]
            cols = [jnp.exp2(s[rows, j * LANES:(j + 1) * LANES] - m_c) for j in range(width // LANES)]
            l_parts.append(functools.reduce(lambda x, y: x + y, cols))
            p_parts.append(jnp.concatenate(cols, axis=1).astype(BF16))
        l_scr[...] = a * l_scr[...] + jnp.concatenate(l_parts, axis=0)
        vblk = v_ref[0, pl.ds(start, width), :]
        pv = jnp.dot(jnp.concatenate(p_parts, axis=0), vblk, preferred_element_type=F32)
        acc_scr[...] = a * acc_scr[...] + pv

    def q_tile(i, carry_unused):
        q = q_ref[0, pl.ds(pl.multiple_of(i * qb, qb), qb), :]
        zero = jnp.zeros_like(q)
        qq_scr[0:qb, :] = jnp.where(lane < HEAD_DIM, q, zero)
        qq_scr[qb:2 * qb, :] = jnp.where(lane >= HEAD_DIM, q, zero)
        m_scr[...] = jnp.full(m_scr.shape, NEG_BIG, F32)
        l_scr[...] = jnp.zeros(l_scr.shape, F32)
        acc_scr[...] = jnp.zeros(acc_scr.shape, F32)

        tiles_per_block = kb // qb

        def full_block(j, c):
            kv_step(j * kb, kb)
            return c

        lax.fori_loop(0, i // tiles_per_block, full_block, 0)

        for r in range(tiles_per_block):
            @pl.when(i % tiles_per_block == r)
            def _(r=r):
                kv_step((i - r) * qb, (r + 1) * qb, diag_tile=True)

        o = acc_scr[...] / jnp.sum(l_scr[...], axis=-1, keepdims=True)
        od = o[:qb] - lam * o[qb:]
        ms = jnp.mean(od * od, axis=-1, keepdims=True)
        od = od * lax.rsqrt(ms + RMS_EPS) * g_ref[...] * (1.0 - lam_init)
        o_ref[0, pl.ds(pl.multiple_of(i * qb, qb), qb), :] = od.astype(BF16)
        return carry_unused

    lax.fori_loop(0, n_qt, q_tile, 0)


def _diff_attention(q, k, v, lam_params, subln_g, *, lam_init, qb):
    b, s, width = q.shape
    hw = 2 * HEAD_DIM
    kb = 4 * qb
    seq_block = pl.BlockSpec((1, s, hw), lambda bi, h: (bi, 0, h))
    return pl.pallas_call(
        functools.partial(_attn_body, lam_init=lam_init, n_qt=s // qb, qb=qb, kb=kb),
        grid=(b, N_HEADS),
        in_specs=[
            pl.BlockSpec((4, HEAD_DIM), lambda bi, h: (0, 0)),
            pl.BlockSpec((1, hw), lambda bi, h: (0, 0)),
            seq_block, seq_block, seq_block,
        ],
        out_specs=seq_block,
        out_shape=jax.ShapeDtypeStruct((b, s, width), BF16),
        scratch_shapes=[
            pltpu.VMEM((2 * qb, hw), BF16),
            pltpu.VMEM((2 * qb, LANES), F32),
            pltpu.VMEM((2 * qb, LANES), F32),
            pltpu.VMEM((2 * qb, hw), F32),
        ],
        compiler_params=pltpu.CompilerParams(
            dimension_semantics=("parallel", "parallel"), vmem_limit_bytes=VMEM_LIMIT),
        name="diff_attn",
    )(lam_params, subln_g, q, k, v)


SCAN_UNROLL = 4


def _ssm_body(u_ref, bbd_ref, cbd_ref, lre_ref, lim_ref, d_ref, wg_ref, bg_ref, z_ref,
              scr, hst, ysc, *, tb, tbp, n_lt):
    nb = u_ref.shape[0]
    n_half = 4

    @pl.when(pl.program_id(0) == 0)
    def _():
        hst[...] = jnp.zeros_like(hst)

    for lt in range(n_lt):
        lanes = slice(lt * LANES, (lt + 1) * LANES)
        for b in range(nb):
            ub = u_ref[b, :, lanes].astype(BF16)
            bu = jnp.dot(ub, bbd_ref[lt], preferred_element_type=F32)
            for k in range(2 * n_half):
                scr[k, b * tbp:b * tbp + tb, :] = bu[:, k * LANES:(k + 1) * LANES]

        lr = [jnp.broadcast_to(lre_ref[lt, :, k * LANES:(k + 1) * LANES], (nb, LANES))
              for k in range(n_half)]
        li = [jnp.broadcast_to(lim_ref[lt, :, k * LANES:(k + 1) * LANES], (nb, LANES))
              for k in range(n_half)]

        def step(t, h):
            new_re, new_im = [], []
            for k in range(n_half):
                rows = pl.ds(t, nb, stride=tbp)
                hr, hi = h[k], h[n_half + k]
                nr = lr[k] * hr - li[k] * hi + scr[k, rows, :]
                ni = lr[k] * hi + li[k] * hr + scr[n_half + k, rows, :]
                scr[k, rows, :] = nr
                scr[n_half + k, rows, :] = ni
                new_re.append(nr)
                new_im.append(ni)
            return tuple(new_re + new_im)

        h = lax.fori_loop(0, tb, step, tuple(hst[lt, k] for k in range(2 * n_half)),
                          unroll=SCAN_UNROLL)
        for k in range(2 * n_half):
            hst[lt, k] = h[k]

        for b in range(nb):
            hb = jnp.concatenate(
                [scr[k, b * tbp:b * tbp + tb, :] for k in range(2 * n_half)], axis=1)
            yb = jnp.dot(hb.astype(BF16), cbd_ref[lt], preferred_element_type=F32)
            ysc[b, :, lanes] = yb + d_ref[:, lanes] * u_ref[b, :, lanes]

    for b in range(nb):
        z = jax.nn.gelu(ysc[b])
        gate = jax.nn.sigmoid(
            jnp.dot(z.astype(BF16), wg_ref[...], preferred_element_type=F32) + bg_ref[...])
        z_ref[b] = (z * gate).astype(BF16)


def _ssm_params(a_re, a_im, log_dt, b_re, b_im, c_re, c_im):
    g = a_re.shape[0]
    n_lt = g * GROUP_CH // LANES
    gpt = g // n_lt
    dt = jnp.exp(log_dt.astype(F32))[:, None]
    ar, ai = a_re.astype(F32), a_im.astype(F32)
    mag = jnp.exp(ar * dt)
    lam_re = mag * jnp.cos(ai * dt)
    lam_im = mag * jnp.sin(ai * dt)
    nr, ni = lam_re - 1.0, lam_im
    den = ar * ar + ai * ai
    cr = (nr * ar + ni * ai) / den
    ci = (ni * ar - nr * ai) / den
    bb_re = cr[..., None] * b_re - ci[..., None] * b_im
    bb_im = cr[..., None] * b_im + ci[..., None] * b_re
    eye = jnp.eye(gpt, dtype=F32)

    def pack_b(bb):
        bb = bb.reshape(n_lt, gpt, SSM_STATE, GROUP_CH)
        m = jnp.einsum('lgpc,gh->lgchp', bb, eye)
        return m.reshape(n_lt, gpt * GROUP_CH, gpt * SSM_STATE)

    def pack_c(cc):
        cc = cc.astype(F32).reshape(n_lt, gpt, GROUP_CH, SSM_STATE)
        m = jnp.einsum('lgcp,gh->lgphc', cc, eye)
        return m.reshape(n_lt, gpt * SSM_STATE, gpt * GROUP_CH)

    bbd = jnp.concatenate([pack_b(bb_re), pack_b(bb_im)], axis=2).astype(BF16)
    cbd = jnp.concatenate([pack_c(c_re), -pack_c(c_im)], axis=1).astype(BF16)
    lre = lam_re.reshape(n_lt, 1, gpt * SSM_STATE)
    lim = lam_im.reshape(n_lt, 1, gpt * SSM_STATE)
    return bbd, cbd, lre, lim


def _ssm_glu(u, bbd, cbd, lre, lim, d, w_glu_bf16, b_glu, *, tb):
    b, s, width = u.shape
    assert b == SUBLANES
    n_lt = width // LANES
    tbp = tb + SUBLANES
    full3 = lambda a: pl.BlockSpec(a.shape, lambda i: (0, 0, 0))
    full2 = lambda a: pl.BlockSpec(a.shape, lambda i: (0, 0))
    return pl.pallas_call(
        functools.partial(_ssm_body, tb=tb, tbp=tbp, n_lt=n_lt),
        grid=(s // tb,),
        in_specs=[
            pl.BlockSpec((b, tb, width), lambda i: (0, i, 0)),
            full3(bbd), full3(cbd), full3(lre), full3(lim), full2(d), full2(w_glu_bf16),
            full2(b_glu),
        ],
        out_specs=pl.BlockSpec((b, tb, width), lambda i: (0, i, 0)),
        out_shape=jax.ShapeDtypeStruct((b, s, width), BF16),
        scratch_shapes=[
            pltpu.VMEM((8, b * tbp, LANES), F32),
            pltpu.VMEM((n_lt, 8, b, LANES), F32),
            pltpu.VMEM((b, tb, width), F32),
        ],
        compiler_params=pltpu.CompilerParams(
            dimension_semantics=("arbitrary",), vmem_limit_bytes=VMEM_LIMIT),
        name="ssm_glu",
    )(u, bbd, cbd, lre, lim, d, w_glu_bf16, b_glu)


def _layernorm(y, g, b):
    mu = jnp.mean(y, axis=-1, keepdims=True)
    yc = y - mu
    var = jnp.mean(yc * yc, axis=-1, keepdims=True)
    return yc * lax.rsqrt(var + LN_EPS) * g + b


N_ROUTE_COLS = N_EXPERT_GROUPS + N_EXPERTS


def _store_token_tiles(ref, val):
    n = val.shape[0]
    for j in range(SUBLANES):
        ref[pl.ds(j, n, stride=SUBLANES), :] = val[:, j * LANES:(j + 1) * LANES]


def _load_token_tiles(ref, n):
    return jnp.concatenate(
        [ref[pl.ds(j, n, stride=SUBLANES), :] for j in range(SUBLANES)], axis=1)


OUT_PROJ_SPLITS = 2


def _out_proj_body(a_ref, z_ref, x_ref, wo_ref, g_ref, b_ref, wr_ref, h8_ref, route_ref, *, alpha):
    n = a_ref.shape[1] // OUT_PROJ_SPLITS
    for part in range(OUT_PROJ_SPLITS):
        rows = pl.ds(part * n, n)
        _out_proj_rows(a_ref[0, rows, :], z_ref[0, rows, :], x_ref[0, rows, :], wo_ref, g_ref,
                       b_ref, wr_ref, h8_ref.at[pl.ds(part * n * SUBLANES, n * SUBLANES)],
                       route_ref.at[0, rows], alpha=alpha)


def _out_proj_rows(a, z, x, wo_ref, g_ref, b_ref, wr_ref, h8_ref, route_ref, *, alpha):
    aw = a.shape[1]
    mix = jnp.dot(a, wo_ref[0:aw, :], preferred_element_type=F32)
    mix += jnp.dot(z, wo_ref[aw:, :], preferred_element_type=F32)
    h = _layernorm(alpha * x + mix, g_ref[...], b_ref[...])
    _store_token_tiles(h8_ref, h)

    h_hi = h.astype(BF16)
    h_lo = (h - h_hi.astype(F32)).astype(BF16)
    lg = jnp.dot(h_hi, wr_ref[...], preferred_element_type=F32)
    lg += jnp.dot(h_lo, wr_ref[...], preferred_element_type=F32)
    lg = lg + pltpu.roll(lg, LANES - N_ROUTE_COLS, 1)

    lane = lax.broadcasted_iota(I32, lg.shape, 1)

    def argmax_lane(vals):
        mx = jnp.max(vals, axis=-1, keepdims=True)
        idx = jnp.min(jnp.where(vals == mx, lane, LANES), axis=-1, keepdims=True)
        return mx, idx

    gl = jnp.where(lane < N_EXPERT_GROUPS, lg, NEG_BIG)
    g_max, g_idx = argmax_lane(gl)
    g_p = 1.0 / jnp.sum(jnp.exp(gl - g_max), axis=-1, keepdims=True)

    e_lo = N_EXPERT_GROUPS + g_idx * EXPERTS_PER_GROUP
    el = jnp.where((lane >= e_lo) & (lane < e_lo + EXPERTS_PER_GROUP), lg, NEG_BIG)
    e1, i1 = argmax_lane(el)
    e2, i2 = argmax_lane(jnp.where(lane == i1, NEG_BIG, el))
    r = jnp.exp(e2 - e1)
    w1 = g_p / (1.0 + r)
    w2 = g_p * r / (1.0 + r)
    route = jnp.where(lane == 0, (i1 - N_EXPERT_GROUPS).astype(F32), 0.0)
    route = jnp.where(lane == 1, (i2 - N_EXPERT_GROUPS).astype(F32), route)
    route = jnp.where(lane == 2, w1, route)
    route = jnp.where(lane == 3, w2, route)
    route_ref[...] = route


def _out_proj(a, z, x, w_out_bf16, ln_g, ln_b, w_route, *, alpha, ts):
    b, s, d = x.shape
    aw = a.shape[2]
    half = pl.BlockSpec((1, ts, aw), lambda bi, i: (bi, i, 0))
    row = pl.BlockSpec((1, ts, d), lambda bi, i: (bi, i, 0))
    full2 = lambda arr: pl.BlockSpec(arr.shape, lambda bi, i: (0, 0))
    assert d == SUBLANES * LANES
    n_st = s // ts
    return pl.pallas_call(
        functools.partial(_out_proj_body, alpha=alpha),
        grid=(b, n_st),
        in_specs=[half, half, row, full2(w_out_bf16), full2(ln_g), full2(ln_b), full2(w_route)],
        out_specs=[
            pl.BlockSpec((ts * SUBLANES, LANES), lambda bi, i: (bi * n_st + i, 0)),
            pl.BlockSpec((1, ts, LANES), lambda bi, i: (bi, i, 0)),
        ],
        out_shape=[
            jax.ShapeDtypeStruct((b * s * SUBLANES, LANES), F32),
            jax.ShapeDtypeStruct((b, s, LANES), F32),
        ],
        compiler_params=pltpu.CompilerParams(
            dimension_semantics=("parallel", "parallel"), vmem_limit_bytes=VMEM_LIMIT),
        name="out_proj_router",
    )(a, z, x, w_out_bf16, ln_g, ln_b, w_route)


ISSUE_UNROLL = 8


def _moe_body(be_ref, src_ref, nv_ref, nused_ref, grow_ref, srow_ref, h8_hbm, wg_ref, wu_ref,
              wd_ref, out_hbm, xbuf, ybuf, wgb, wub, wdb, gsem, ssem, *, bm, n_tok):
    i = pl.program_id(0)
    n_used = nused_ref[0]
    n_pairs = 2 * n_tok
    tile = SUBLANES

    def tile_row(r):
        return r * tile if isinstance(r, int) else pl.multiple_of(r * tile, tile)

    def gather_copy(blk, buf_slot, r):
        src_row = grow_ref[src_ref[blk] + r]
        return pltpu.make_async_copy(
            h8_hbm.at[pl.ds(pl.multiple_of(src_row, tile), tile)],
            xbuf.at[buf_slot, pl.ds(tile_row(r), tile)],
            gsem.at[buf_slot])

    def scatter_copy(blk, buf_slot, r):
        blk_c = jnp.maximum(blk, 0)
        n_valid = jnp.where(blk >= 0, nv_ref[blk_c], 0)
        real_row = srow_ref[src_ref[blk_c] + r]
        pad_row = (n_pairs + buf_slot * bm + r) * tile
        dst_row = jnp.where(r < n_valid, real_row, pad_row)
        return pltpu.make_async_copy(
            ybuf.at[buf_slot, pl.ds(tile_row(r), tile)],
            out_hbm.at[pl.ds(pl.multiple_of(dst_row, tile), tile)],
            ssem.at[buf_slot])

    def issue_loop(make_copy, blk, buf_slot):
        def trip(g, c):
            for q in range(ISSUE_UNROLL):
                make_copy(blk, buf_slot, pl.multiple_of(g * ISSUE_UNROLL, ISSUE_UNROLL) + q).start()
            return c
        lax.fori_loop(0, bm // ISSUE_UNROLL, trip, 0)

    def wait_gather(buf_slot):
        pltpu.make_async_copy(h8_hbm.at[pl.ds(0, bm * tile)], xbuf.at[buf_slot],
                              gsem.at[buf_slot]).wait()

    def wait_scatter(buf_slot):
        pltpu.make_async_copy(ybuf.at[buf_slot], out_hbm.at[pl.ds(0, bm * tile)],
                              ssem.at[buf_slot]).wait()

    @pl.when(i == 0)
    def _():
        issue_loop(gather_copy, 0, 0)
        ybuf[...] = jnp.zeros_like(ybuf)
        for s in range(2):
            pad_rows = out_hbm.at[pl.ds((n_pairs + s * bm) * tile, bm * tile)]
            pltpu.make_async_copy(ybuf.at[s], pad_rows, ssem.at[s]).start()
            pltpu.make_async_copy(ybuf.at[s], pad_rows, ssem.at[s]).wait()

    def block_step(slot):
        other = 1 - slot
        wait_gather(slot)

        @pl.when(i >= 1)
        def _():
            wait_scatter(slot)

        prev = be_ref[jnp.maximum(i - 1, 0)]

        @pl.when((i == 0) | (be_ref[i] != prev))
        def _():
            wgb[...] = wg_ref[0].astype(BF16)
            wub[...] = wu_ref[0].astype(BF16)
            wdb[...] = wd_ref[0].astype(BF16)

        x = _load_token_tiles(xbuf.at[slot], bm).astype(BF16)
        nxt = jnp.minimum(i + 1, n_used - 1)
        for r in range(bm):
            gather_copy(nxt, other, r).start()
            scatter_copy(i - 1, other, r).start()
        gate = jnp.dot(x, wgb[...], preferred_element_type=F32)
        up = jnp.dot(x, wub[...], preferred_element_type=F32)
        hid = (jax.nn.silu(gate) * up).astype(BF16)
        y = jnp.dot(hid, wdb[...], preferred_element_type=F32)
        _store_token_tiles(ybuf.at[slot], y)

    for s in range(2):
        pl.when((i < n_used) & (i % 2 == s))(functools.partial(block_step, s))

    @pl.when(i == n_used)
    def _():
        last_slot = (n_used - 1) % 2
        issue_loop(scatter_copy, n_used - 1, last_slot)
        wait_scatter(0)
        wait_scatter(1)
        pltpu.make_async_copy(h8_hbm.at[pl.ds(0, bm * tile)], xbuf.at[1 - last_slot],
                              gsem.at[1 - last_slot]).wait()


def _moe_ffn(block_e, block_src, block_nv, n_used, gather_row, scatter_row, h8, w_gate, w_up,
             w_down, *, bm):
    n_tok = h8.shape[0] // SUBLANES
    d = SUBLANES * LANES
    n_blocks = block_e.shape[0]
    de = w_gate.shape[2]
    out_rows = (2 * n_tok + 2 * bm) * SUBLANES
    wspec = lambda shape: pl.BlockSpec(shape, lambda i, be, *_: (be[i], 0, 0))
    grid_spec = pltpu.PrefetchScalarGridSpec(
        num_scalar_prefetch=6,
        grid=(n_blocks,),
        in_specs=[
            pl.BlockSpec(memory_space=pl.ANY),
            wspec((1, d, de)), wspec((1, d, de)), wspec((1, de, d)),
        ],
        out_specs=pl.BlockSpec(memory_space=pl.ANY),
        scratch_shapes=[
            pltpu.VMEM((2, bm * SUBLANES, LANES), F32),
            pltpu.VMEM((2, bm * SUBLANES, LANES), F32),
            pltpu.VMEM((d, de), BF16),
            pltpu.VMEM((d, de), BF16),
            pltpu.VMEM((de, d), BF16),
            pltpu.SemaphoreType.DMA((2,)),
            pltpu.SemaphoreType.DMA((2,)),
        ],
    )
    return pl.pallas_call(
        functools.partial(_moe_body, bm=bm, n_tok=n_tok),
        grid_spec=grid_spec,
        out_shape=jax.ShapeDtypeStruct((out_rows, LANES), F32),
        compiler_params=pltpu.CompilerParams(
            dimension_semantics=("arbitrary",), vmem_limit_bytes=VMEM_LIMIT),
        name="moe_ffn",
    )(block_e, block_src, block_nv, n_used, gather_row, scatter_row, h8, w_gate, w_up, w_down)


def _combine_body(h8_ref, y0_ref, y1_ref, route_ref, g_ref, b_ref, o_ref, *, tt, alpha):
    route = route_ref[...]
    w0 = route[:, 2:3]
    w1 = route[:, 3:4]
    h = _load_token_tiles(h8_ref, tt)
    moe = w0 * _load_token_tiles(y0_ref, tt) + w1 * _load_token_tiles(y1_ref, tt)
    o_ref[...] = _layernorm(alpha * h + moe, g_ref[...], b_ref[...])


def _combine(h8, out2, route, ln_g, ln_b, *, alpha, tt):
    t = route.shape[0]
    d = SUBLANES * LANES
    n_tiles = t // tt
    tile_rows = pl.BlockSpec((tt * SUBLANES, LANES), lambda i: (i, 0))
    return pl.pallas_call(
        functools.partial(_combine_body, tt=tt, alpha=alpha),
        grid=(n_tiles,),
        in_specs=[
            tile_rows,
            tile_rows,
            pl.BlockSpec((tt * SUBLANES, LANES), lambda i: (n_tiles + i, 0)),
            pl.BlockSpec((tt, LANES), lambda i: (i, 0)),
            pl.BlockSpec((1, d), lambda i: (0, 0)),
            pl.BlockSpec((1, d), lambda i: (0, 0)),
        ],
        out_specs=pl.BlockSpec((tt, d), lambda i: (i, 0)),
        out_shape=jax.ShapeDtypeStruct((t, d), F32),
        compiler_params=pltpu.CompilerParams(
            dimension_semantics=("parallel",), vmem_limit_bytes=VMEM_LIMIT),
        name="moe_combine",
    )(h8, out2, out2, route, ln_g, ln_b)


def _moe_layout(route, *, bm):
    t = route.shape[0]
    tk = 2 * t
    flat_e = route[:, 0:2].astype(I32).reshape(-1)
    key = jnp.sort(flat_e * tk + jnp.arange(tk, dtype=I32))
    order = key % tk
    experts = jnp.arange(N_EXPERTS, dtype=I32)
    counts = jnp.sum((flat_e[:, None] == experts[None, :]).astype(I32), axis=0)
    starts = jnp.cumsum(counts) - counts
    pcounts = (counts + bm - 1) // bm * bm
    pends = jnp.cumsum(pcounts)
    pstarts = pends - pcounts
    n_blocks = (tk + N_EXPERTS * bm) // bm
    row0 = jnp.arange(n_blocks, dtype=I32) * bm
    block_e = jnp.minimum(
        jnp.sum((pends[None, :] <= row0[:, None]).astype(I32), axis=1), N_EXPERTS - 1)
    off = row0 - pstarts[block_e]
    block_src = starts[block_e] + off
    block_nv = jnp.clip(counts[block_e] - off, 0, bm)
    n_used = (pends[-1] // bm).astype(I32).reshape(1)
    spare = jnp.zeros((bm,), I32)
    gather_row = jnp.concatenate([(order // 2) * SUBLANES, spare])
    scatter_row = jnp.concatenate([((order % 2) * t + order // 2) * SUBLANES, spare])
    return (block_e, block_src.astype(I32), block_nv.astype(I32), n_used, gather_row.astype(I32),
            scatter_row.astype(I32))


def _layer(x, w_in, lam_q1, lam_k1, lam_q2, lam_k2, subln_g, a_re, a_im, log_dt, b_re, b_im,
           c_re, c_im, d, w_glu, b_glu, w_out, ln1_g, ln1_b, w_rg, w_re, w_gate, w_up, w_down,
           ln2_g, ln2_b, *, layer_idx, alpha, ts, qb, tb, bm, tt):
    b, s, dm = x.shape
    lam_init = 0.8 - 0.6 * math.exp(-0.3 * layer_idx)

    q, k, v, u = _in_proj(x, w_in.astype(BF16), ts=ts)

    lam_params = jnp.stack([lam_q1, lam_k1, lam_q2, lam_k2]).astype(F32)
    a_out = _diff_attention(q, k, v, lam_params, subln_g.reshape(1, -1).astype(F32),
                            lam_init=lam_init, qb=qb)

    bbd, cbd, lre, lim = _ssm_params(a_re, a_im, log_dt, b_re, b_im, c_re, c_im)
    s_out = _ssm_glu(u, bbd, cbd, lre, lim, d.reshape(1, -1).astype(F32), w_glu.astype(BF16),
                     b_glu.reshape(1, -1).astype(F32), tb=tb)

    w_r = jnp.concatenate([w_rg, w_re], axis=1).astype(F32)
    w_r_hi = w_r.astype(BF16)
    w_r_lo = (w_r - w_r_hi.astype(F32)).astype(BF16)
    w_route = jnp.concatenate(
        [w_r_hi, w_r_lo, jnp.zeros((dm, LANES - 2 * N_ROUTE_COLS), BF16)], axis=1)
    h8, route = _out_proj(a_out, s_out, x, w_out.astype(BF16), ln1_g.reshape(1, -1),
                          ln1_b.reshape(1, -1), w_route, alpha=alpha, ts=ts)

    route = route.reshape(b * s, LANES)
    out2 = _moe_ffn(*_moe_layout(route, bm=bm), h8, w_gate, w_up, w_down, bm=bm)
    out = _combine(h8, out2, route, ln2_g.reshape(1, -1), ln2_b.reshape(1, -1), alpha=alpha, tt=tt)
    return out.reshape(b, s, dm)


def kernel(x, w_in, lam_q1, lam_k1, lam_q2, lam_k2, subln_g, ssm_a_re, ssm_a_im, ssm_log_dt, ssm_b_re, ssm_b_im, ssm_c_re, ssm_c_im, ssm_d, w_glu, b_glu, w_out, ln1_g, ln1_b, w_router_group, w_router_expert, w_exp_gate, w_exp_up, w_exp_down, ln2_g, ln2_b):
    depth = w_in.shape[0]
    alpha = (2.0 * depth) ** 0.25
    for i in range(depth):
        x = _layer(
            x, w_in[i], lam_q1[i], lam_k1[i], lam_q2[i], lam_k2[i], subln_g[i], ssm_a_re[i],
            ssm_a_im[i], ssm_log_dt[i], ssm_b_re[i], ssm_b_im[i], ssm_c_re[i], ssm_c_im[i],
            ssm_d[i], w_glu[i], b_glu[i], w_out[i], ln1_g[i], ln1_b[i], w_router_group[i],
            w_router_expert[i], w_exp_gate[i], w_exp_up[i], w_exp_down[i], ln2_g[i], ln2_b[i],
            layer_idx=i, alpha=alpha, ts=512, qb=256, tb=256, bm=256, tt=512)
    return x
```

```python
import functools
import math

import jax
import jax.numpy as jnp
from jax import lax
from jax.experimental import pallas as pl
from jax.experimental.pallas import tpu as pltpu

F32 = jnp.float32
BF16 = jnp.bfloat16
I32 = jnp.int32

DEPTH = 1
N_HEADS = 4
HEAD_DIM = 64
ATTN_WIDTH = N_HEADS * 2 * HEAD_DIM
CHUNK = 64
GROUP_CH = 16
SSM_STATE = 64
N_EXPERT_GROUPS = 4
EXPERTS_PER_GROUP = 8
N_EXPERTS = N_EXPERT_GROUPS * EXPERTS_PER_GROUP
LN_EPS = 1e-5
RMS_EPS = 1e-5

LANES = 128
SUBLANES = 8
VMEM_LIMIT = 56 * 1024 * 1024

NEG_BIG = -1e30


def _in_proj_body(x_ref, w_ref, q_ref, k_ref, v_ref, u_ref, *, width, q_scale):
    x = x_ref[0].astype(BF16)

    def proj(c):
        return jnp.dot(x, w_ref[:, c * width:(c + 1) * width], preferred_element_type=F32)

    q_ref[0] = (proj(0) * q_scale).astype(BF16)
    k_ref[0] = proj(1).astype(BF16)
    v_ref[0] = proj(2).astype(BF16)
    u_ref[0] = proj(3)


def _in_proj(x, w_in_bf16, *, ts):
    b, s, d = x.shape
    width = ATTN_WIDTH
    out_block = pl.BlockSpec((1, ts, width), lambda bi, i: (bi, i, 0))
    return pl.pallas_call(
        functools.partial(_in_proj_body, width=width, q_scale=HEAD_DIM ** -0.5 * math.log2(math.e)),
        grid=(b, s // ts),
        in_specs=[
            pl.BlockSpec((1, ts, d), lambda bi, i: (bi, i, 0)),
            pl.BlockSpec((d, 4 * width), lambda bi, i: (0, 0)),
        ],
        out_specs=[out_block, out_block, out_block, out_block],
        out_shape=[
            jax.ShapeDtypeStruct((b, s, width), BF16),
            jax.ShapeDtypeStruct((b, s, width), BF16),
            jax.ShapeDtypeStruct((b, s, width), BF16),
            jax.ShapeDtypeStruct((b, s, width), F32),
        ],
        compiler_params=pltpu.CompilerParams(
            dimension_semantics=("parallel", "parallel"), vmem_limit_bytes=VMEM_LIMIT),
        name="in_proj",
    )(x, w_in_bf16)


ATTN_ROWS = 16
ATTN_TILES_PER_STEP = 8


def _attn_body(lamp_ref, g_ref, q_ref, k_ref, v_ref, o_ref,
               qq_scr, m_scr, l_scr, acc_scr, *, lam_init, n_qt, qb, kb):
    lp = lamp_ref[...]
    s1 = jnp.sum(lp[0:1] * lp[1:2], axis=-1, keepdims=True)
    s2 = jnp.sum(lp[2:3] * lp[3:4], axis=-1, keepdims=True)
    lam = jnp.exp(s1) - jnp.exp(s2) + lam_init

    lane = lax.broadcasted_iota(I32, (qb, 2 * HEAD_DIM), 1)
    n_sub = 2 * qb // ATTN_ROWS

    def kv_step(start, width, diag_tile=False):
        start = pl.multiple_of(start, qb)
        kblk = k_ref[0, pl.ds(start, width), :]
        s = lax.dot_general(
            qq_scr[...], kblk, (((1,), (1,)), ((), ())), preferred_element_type=F32)
        if diag_tile:
            q_row = lax.broadcasted_iota(I32, (2 * qb, qb), 0) % qb
            k_col = lax.broadcasted_iota(I32, (2 * qb, qb), 1)
            diag = jnp.where(k_col // CHUNK <= q_row // CHUNK, s[:, width - qb:], NEG_BIG)
            s = diag if width == qb else jnp.concatenate([s[:, :width - qb], diag], axis=1)
        m_old = m_scr[...]
        m_new = jnp.maximum(m_old, jnp.max(s, axis=-1, keepdims=True))
        a = jnp.exp2(m_old - m_new)
        m_scr[...] = m_new
        l_parts, p_parts = [], []
        for c in range(n_sub):
            rows = slice(c * ATTN_ROWS, (c + 1) * ATTN_ROWS)
            m_c = m_new[rows]
            cols = [jnp.exp2(s[rows, j * LANES:(j + 1) * LANES] - m_c) for j in range(width // LANES)]
            l_parts.append(functools.reduce(lambda x, y: x + y, cols))
            p_parts.append(jnp.concatenate(cols, axis=1).astype(BF16))
        l_scr[...] = a * l_scr[...] + jnp.concatenate(l_parts, axis=0)
        vblk = v_ref[0, pl.ds(start, width), :]
        pv = jnp.dot(jnp.concatenate(p_parts, axis=0), vblk, preferred_element_type=F32)
        acc_scr[...] = a * acc_scr[...] + pv

    def q_tile(i, carry_unused):
        q = q_ref[0, pl.ds(pl.multiple_of(i * qb, qb), qb), :]
        zero = jnp.zeros_like(q)
        qq_scr[0:qb, :] = jnp.where(lane < HEAD_DIM, q, zero)
        qq_scr[qb:2 * qb, :] = jnp.where(lane >= HEAD_DIM, q, zero)
        m_scr[...] = jnp.full(m_scr.shape, NEG_BIG, F32)
        l_scr[...] = jnp.zeros(l_scr.shape, F32)
        acc_scr[...] = jnp.zeros(acc_scr.shape, F32)

        tiles_per_block = kb // qb

        def full_block(j, c):
            kv_step(j * kb, kb)
            return c

        lax.fori_loop(0, i // tiles_per_block, full_block, 0)

        for r in range(tiles_per_block):
            @pl.when(i % tiles_per_block == r)
            def _(r=r):
                kv_step((i - r) * qb, (r + 1) * qb, diag_tile=True)

        o = acc_scr[...] / jnp.sum(l_scr[...], axis=-1, keepdims=True)
        od = o[:qb] - lam * o[qb:]
        ms = jnp.mean(od * od, axis=-1, keepdims=True)
        od = od * lax.rsqrt(ms + RMS_EPS) * g_ref[...] * (1.0 - lam_init)
        o_ref[0, pl.ds(pl.multiple_of(i * qb, qb), qb), :] = od.astype(BF16)
        return carry_unused

    lax.fori_loop(0, n_qt, q_tile, 0)


def _diff_attention(q, k, v, lam_params, subln_g, *, lam_init, qb):
    b, s, width = q.shape
    hw = 2 * HEAD_DIM
    kb = ATTN_TILES_PER_STEP * qb
    seq_block = pl.BlockSpec((1, s, hw), lambda bi, h: (bi, 0, h))
    return pl.pallas_call(
        functools.partial(_attn_body, lam_init=lam_init, n_qt=s // qb, qb=qb, kb=kb),
        grid=(b, N_HEADS),
        in_specs=[
            pl.BlockSpec((4, HEAD_DIM), lambda bi, h: (0, 0)),
            pl.BlockSpec((1, hw), lambda bi, h: (0, 0)),
            seq_block, seq_block, seq_block,
        ],
        out_specs=seq_block,
        out_shape=jax.ShapeDtypeStruct((b, s, width), BF16),
        scratch_shapes=[
            pltpu.VMEM((2 * qb, hw), BF16),
            pltpu.VMEM((2 * qb, LANES), F32),
            pltpu.VMEM((2 * qb, LANES), F32),
            pltpu.VMEM((2 * qb, hw), F32),
        ],
        compiler_params=pltpu.CompilerParams(
            dimension_semantics=("parallel", "parallel"), vmem_limit_bytes=VMEM_LIMIT),
        name="diff_attn",
    )(lam_params, subln_g, q, k, v)


SCAN_UNROLL = 4


def _ssm_body(u_ref, bbd_ref, cbd_ref, lre_ref, lim_ref, d_ref, wg_ref, bg_ref, z_ref,
              scr, hst, ysc, *, tb, tbp, n_lt):
    nb = u_ref.shape[0]
    n_half = 4

    @pl.when(pl.program_id(0) == 0)
    def _():
        hst[...] = jnp.zeros_like(hst)

    for lt in range(n_lt):
        lanes = slice(lt * LANES, (lt + 1) * LANES)
        for b in range(nb):
            ub = u_ref[b, :, lanes].astype(BF16)
            bu = jnp.dot(ub, bbd_ref[lt], preferred_element_type=F32)
            for k in range(2 * n_half):
                scr[k, b * tbp:b * tbp + tb, :] = bu[:, k * LANES:(k + 1) * LANES]

        lr = [jnp.broadcast_to(lre_ref[lt, :, k * LANES:(k + 1) * LANES], (nb, LANES))
              for k in range(n_half)]
        li = [jnp.broadcast_to(lim_ref[lt, :, k * LANES:(k + 1) * LANES], (nb, LANES))
              for k in range(n_half)]

        def step(t, h):
            new_re, new_im = [], []
            for k in range(n_half):
                rows = pl.ds(t, nb, stride=tbp)
                hr, hi = h[k], h[n_half + k]
                nr = lr[k] * hr - li[k] * hi + scr[k, rows, :]
                ni = lr[k] * hi + li[k] * hr + scr[n_half + k, rows, :]
                scr[k, rows, :] = nr
                scr[n_half + k, rows, :] = ni
                new_re.append(nr)
                new_im.append(ni)
            return tuple(new_re + new_im)

        h = lax.fori_loop(0, tb, step, tuple(hst[lt, k] for k in range(2 * n_half)),
                          unroll=SCAN_UNROLL)
        for k in range(2 * n_half):
            hst[lt, k] = h[k]

        for b in range(nb):
            hb = jnp.concatenate(
                [scr[k, b * tbp:b * tbp + tb, :] for k in range(2 * n_half)], axis=1)
            yb = jnp.dot(hb.astype(BF16), cbd_ref[lt], preferred_element_type=F32)
            ysc[b, :, lanes] = yb + d_ref[:, lanes] * u_ref[b, :, lanes]

    for b in range(nb):
        z = jax.nn.gelu(ysc[b])
        gate = jax.nn.sigmoid(
            jnp.dot(z.astype(BF16), wg_ref[...], preferred_element_type=F32) + bg_ref[...])
        z_ref[b] = (z * gate).astype(BF16)


def _ssm_params(a_re, a_im, log_dt, b_re, b_im, c_re, c_im):
    g = a_re.shape[0]
    n_lt = g * GROUP_CH // LANES
    gpt = g // n_lt
    dt = jnp.exp(log_dt.astype(F32))[:, None]
    ar, ai = a_re.astype(F32), a_im.astype(F32)
    mag = jnp.exp(ar * dt)
    lam_re = mag * jnp.cos(ai * dt)
    lam_im = mag * jnp.sin(ai * dt)
    nr, ni = lam_re - 1.0, lam_im
    den = ar * ar + ai * ai
    cr = (nr * ar + ni * ai) / den
    ci = (ni * ar - nr * ai) / den
    bb_re = cr[..., None] * b_re - ci[..., None] * b_im
    bb_im = cr[..., None] * b_im + ci[..., None] * b_re
    eye = jnp.eye(gpt, dtype=F32)

    def pack_b(bb):
        bb = bb.reshape(n_lt, gpt, SSM_STATE, GROUP_CH)
        m = jnp.einsum('lgpc,gh->lgchp', bb, eye)
        return m.reshape(n_lt, gpt * GROUP_CH, gpt * SSM_STATE)

    def pack_c(cc):
        cc = cc.astype(F32).reshape(n_lt, gpt, GROUP_CH, SSM_STATE)
        m = jnp.einsum('lgcp,gh->lgphc', cc, eye)
        return m.reshape(n_lt, gpt * SSM_STATE, gpt * GROUP_CH)

    bbd = jnp.concatenate([pack_b(bb_re), pack_b(bb_im)], axis=2).astype(BF16)
    cbd = jnp.concatenate([pack_c(c_re), -pack_c(c_im)], axis=1).astype(BF16)
    lre = lam_re.reshape(n_lt, 1, gpt * SSM_STATE)
    lim = lam_im.reshape(n_lt, 1, gpt * SSM_STATE)
    return bbd, cbd, lre, lim


def _ssm_glu(u, bbd, cbd, lre, lim, d, w_glu_bf16, b_glu, *, tb):
    b, s, width = u.shape
    assert b == SUBLANES
    n_lt = width // LANES
    tbp = tb + SUBLANES
    full3 = lambda a: pl.BlockSpec(a.shape, lambda i: (0, 0, 0))
    full2 = lambda a: pl.BlockSpec(a.shape, lambda i: (0, 0))
    return pl.pallas_call(
        functools.partial(_ssm_body, tb=tb, tbp=tbp, n_lt=n_lt),
        grid=(s // tb,),
        in_specs=[
            pl.BlockSpec((b, tb, width), lambda i: (0, i, 0)),
            full3(bbd), full3(cbd), full3(lre), full3(lim), full2(d), full2(w_glu_bf16),
            full2(b_glu),
        ],
        out_specs=pl.BlockSpec((b, tb, width), lambda i: (0, i, 0)),
        out_shape=jax.ShapeDtypeStruct((b, s, width), BF16),
        scratch_shapes=[
            pltpu.VMEM((8, b * tbp, LANES), F32),
            pltpu.VMEM((n_lt, 8, b, LANES), F32),
            pltpu.VMEM((b, tb, width), F32),
        ],
        compiler_params=pltpu.CompilerParams(
            dimension_semantics=("arbitrary",), vmem_limit_bytes=VMEM_LIMIT),
        name="ssm_glu",
    )(u, bbd, cbd, lre, lim, d, w_glu_bf16, b_glu)


def _layernorm(y, g, b):
    mu = jnp.mean(y, axis=-1, keepdims=True)
    yc = y - mu
    var = jnp.mean(yc * yc, axis=-1, keepdims=True)
    return yc * lax.rsqrt(var + LN_EPS) * g + b


N_ROUTE_COLS = N_EXPERT_GROUPS + N_EXPERTS


def _store_token_tiles(ref, val):
    n = val.shape[0]
    for j in range(SUBLANES):
        ref[pl.ds(j, n, stride=SUBLANES), :] = val[:, j * LANES:(j + 1) * LANES]


def _load_token_tiles(ref, n):
    return jnp.concatenate(
        [ref[pl.ds(j, n, stride=SUBLANES), :] for j in range(SUBLANES)], axis=1)


OUT_PROJ_SPLITS = 2


def _out_proj_body(a_ref, z_ref, x_ref, wo_ref, g_ref, b_ref, wr_ref, h8_ref, route_ref, *, alpha):
    n = a_ref.shape[1] // OUT_PROJ_SPLITS
    for part in range(OUT_PROJ_SPLITS):
        rows = pl.ds(part * n, n)
        _out_proj_rows(a_ref[0, rows, :], z_ref[0, rows, :], x_ref[0, rows, :], wo_ref, g_ref,
                       b_ref, wr_ref, h8_ref.at[pl.ds(part * n * SUBLANES, n * SUBLANES)],
                       route_ref.at[0, rows], alpha=alpha)


def _out_proj_rows(a, z, x, wo_ref, g_ref, b_ref, wr_ref, h8_ref, route_ref, *, alpha):
    aw = a.shape[1]
    mix = jnp.dot(a, wo_ref[0:aw, :], preferred_element_type=F32)
    mix += jnp.dot(z, wo_ref[aw:, :], preferred_element_type=F32)
    h = _layernorm(alpha * x + mix, g_ref[...], b_ref[...])
    _store_token_tiles(h8_ref, h)

    h_hi = h.astype(BF16)
    h_lo = (h - h_hi.astype(F32)).astype(BF16)
    lg = jnp.dot(h_hi, wr_ref[...], preferred_element_type=F32)
    lg += jnp.dot(h_lo, wr_ref[...], preferred_element_type=F32)
    lg = lg + pltpu.roll(lg, LANES - N_ROUTE_COLS, 1)

    lane = lax.broadcasted_iota(I32, lg.shape, 1)

    def argmax_lane(vals):
        mx = jnp.max(vals, axis=-1, keepdims=True)
        idx = jnp.min(jnp.where(vals == mx, lane, LANES), axis=-1, keepdims=True)
        return mx, idx

    gl = jnp.where(lane < N_EXPERT_GROUPS, lg, NEG_BIG)
    g_max, g_idx = argmax_lane(gl)
    g_p = 1.0 / jnp.sum(jnp.exp(gl - g_max), axis=-1, keepdims=True)

    e_lo = N_EXPERT_GROUPS + g_idx * EXPERTS_PER_GROUP
    el = jnp.where((lane >= e_lo) & (lane < e_lo + EXPERTS_PER_GROUP), lg, NEG_BIG)
    e1, i1 = argmax_lane(el)
    e2, i2 = argmax_lane(jnp.where(lane == i1, NEG_BIG, el))
    r = jnp.exp(e2 - e1)
    w1 = g_p / (1.0 + r)
    w2 = g_p * r / (1.0 + r)
    route = jnp.where(lane == 0, (i1 - N_EXPERT_GROUPS).astype(F32), 0.0)
    route = jnp.where(lane == 1, (i2 - N_EXPERT_GROUPS).astype(F32), route)
    route = jnp.where(lane == 2, w1, route)
    route = jnp.where(lane == 3, w2, route)
    route_ref[...] = route


def _out_proj(a, z, x, w_out_bf16, ln_g, ln_b, w_route, *, alpha, ts):
    b, s, d = x.shape
    aw = a.shape[2]
    half = pl.BlockSpec((1, ts, aw), lambda bi, i: (bi, i, 0))
    row = pl.BlockSpec((1, ts, d), lambda bi, i: (bi, i, 0))
    full2 = lambda arr: pl.BlockSpec(arr.shape, lambda bi, i: (0, 0))
    assert d == SUBLANES * LANES
    n_st = s // ts
    return pl.pallas_call(
        functools.partial(_out_proj_body, alpha=alpha),
        grid=(b, n_st),
        in_specs=[half, half, row, full2(w_out_bf16), full2(ln_g), full2(ln_b), full2(w_route)],
        out_specs=[
            pl.BlockSpec((ts * SUBLANES, LANES), lambda bi, i: (bi * n_st + i, 0)),
            pl.BlockSpec((1, ts, LANES), lambda bi, i: (bi, i, 0)),
        ],
        out_shape=[
            jax.ShapeDtypeStruct((b * s * SUBLANES, LANES), F32),
            jax.ShapeDtypeStruct((b, s, LANES), F32),
        ],
        compiler_params=pltpu.CompilerParams(
            dimension_semantics=("parallel", "parallel"), vmem_limit_bytes=VMEM_LIMIT),
        name="out_proj_router",
    )(a, z, x, w_out_bf16, ln_g, ln_b, w_route)


ISSUE_UNROLL = 8


def _moe_body(be_ref, src_ref, nv_ref, nused_ref, grow_ref, srow_ref, h8_hbm, wg_ref, wu_ref,
              wd_ref, out_hbm, xbuf, ybuf, wgb, wub, wdb, gsem, ssem, *, bm, n_tok):
    i = pl.program_id(0)
    n_used = nused_ref[0]
    n_pairs = 2 * n_tok
    tile = SUBLANES

    def tile_row(r):
        return r * tile if isinstance(r, int) else pl.multiple_of(r * tile, tile)

    def gather_copy(blk, buf_slot, r):
        src_row = grow_ref[src_ref[blk] + r]
        return pltpu.make_async_copy(
            h8_hbm.at[pl.ds(pl.multiple_of(src_row, tile), tile)],
            xbuf.at[buf_slot, pl.ds(tile_row(r), tile)],
            gsem.at[buf_slot])

    def scatter_copy(blk, buf_slot, r):
        blk_c = jnp.maximum(blk, 0)
        n_valid = jnp.where(blk >= 0, nv_ref[blk_c], 0)
        real_row = srow_ref[src_ref[blk_c] + r]
        pad_row = (n_pairs + buf_slot * bm + r) * tile
        dst_row = jnp.where(r < n_valid, real_row, pad_row)
        return pltpu.make_async_copy(
            ybuf.at[buf_slot, pl.ds(tile_row(r), tile)],
            out_hbm.at[pl.ds(pl.multiple_of(dst_row, tile), tile)],
            ssem.at[buf_slot])

    def issue_loop(make_copy, blk, buf_slot):
        def trip(g, c):
            for q in range(ISSUE_UNROLL):
                make_copy(blk, buf_slot, pl.multiple_of(g * ISSUE_UNROLL, ISSUE_UNROLL) + q).start()
            return c
        lax.fori_loop(0, bm // ISSUE_UNROLL, trip, 0)

    def wait_gather(buf_slot):
        pltpu.make_async_copy(h8_hbm.at[pl.ds(0, bm * tile)], xbuf.at[buf_slot],
                              gsem.at[buf_slot]).wait()

    def wait_scatter(buf_slot):
        pltpu.make_async_copy(ybuf.at[buf_slot], out_hbm.at[pl.ds(0, bm * tile)],
                              ssem.at[buf_slot]).wait()

    @pl.when(i == 0)
    def _():
        issue_loop(gather_copy, 0, 0)
        ybuf[...] = jnp.zeros_like(ybuf)
        for s in range(2):
            pad_rows = out_hbm.at[pl.ds((n_pairs + s * bm) * tile, bm * tile)]
            pltpu.make_async_copy(ybuf.at[s], pad_rows, ssem.at[s]).start()
            pltpu.make_async_copy(ybuf.at[s], pad_rows, ssem.at[s]).wait()

    def block_step(slot):
        other = 1 - slot
        wait_gather(slot)

        @pl.when(i >= 1)
        def _():
            wait_scatter(slot)

        prev = be_ref[jnp.maximum(i - 1, 0)]

        @pl.when((i == 0) | (be_ref[i] != prev))
        def _():
            wgb[...] = wg_ref[0].astype(BF16)
            wub[...] = wu_ref[0].astype(BF16)
            wdb[...] = wd_ref[0].astype(BF16)

        x = _load_token_tiles(xbuf.at[slot], bm).astype(BF16)
        nxt = jnp.minimum(i + 1, n_used - 1)
        for r in range(bm):
            gather_copy(nxt, other, r).start()
            scatter_copy(i - 1, other, r).start()
        gate = jnp.dot(x, wgb[...], preferred_element_type=F32)
        up = jnp.dot(x, wub[...], preferred_element_type=F32)
        hid = (jax.nn.silu(gate) * up).astype(BF16)
        y = jnp.dot(hid, wdb[...], preferred_element_type=F32)
        _store_token_tiles(ybuf.at[slot], y)

    for s in range(2):
        pl.when((i < n_used) & (i % 2 == s))(functools.partial(block_step, s))

    @pl.when(i == n_used)
    def _():
        last_slot = (n_used - 1) % 2
        issue_loop(scatter_copy, n_used - 1, last_slot)
        wait_scatter(0)
        wait_scatter(1)
        pltpu.make_async_copy(h8_hbm.at[pl.ds(0, bm * tile)], xbuf.at[1 - last_slot],
                              gsem.at[1 - last_slot]).wait()


def _moe_ffn(block_e, block_src, block_nv, n_used, gather_row, scatter_row, h8, w_gate, w_up,
             w_down, *, bm):
    n_tok = h8.shape[0] // SUBLANES
    d = SUBLANES * LANES
    n_blocks = block_e.shape[0]
    de = w_gate.shape[2]
    out_rows = (2 * n_tok + 2 * bm) * SUBLANES
    wspec = lambda shape: pl.BlockSpec(shape, lambda i, be, *_: (be[i], 0, 0))
    grid_spec = pltpu.PrefetchScalarGridSpec(
        num_scalar_prefetch=6,
        grid=(n_blocks,),
        in_specs=[
            pl.BlockSpec(memory_space=pl.ANY),
            wspec((1, d, de)), wspec((1, d, de)), wspec((1, de, d)),
        ],
        out_specs=pl.BlockSpec(memory_space=pl.ANY),
        scratch_shapes=[
            pltpu.VMEM((2, bm * SUBLANES, LANES), F32),
            pltpu.VMEM((2, bm * SUBLANES, LANES), F32),
            pltpu.VMEM((d, de), BF16),
            pltpu.VMEM((d, de), BF16),
            pltpu.VMEM((de, d), BF16),
            pltpu.SemaphoreType.DMA((2,)),
            pltpu.SemaphoreType.DMA((2,)),
        ],
    )
    return pl.pallas_call(
        functools.partial(_moe_body, bm=bm, n_tok=n_tok),
        grid_spec=grid_spec,
        out_shape=jax.ShapeDtypeStruct((out_rows, LANES), F32),
        compiler_params=pltpu.CompilerParams(
            dimension_semantics=("arbitrary",), vmem_limit_bytes=VMEM_LIMIT),
        name="moe_ffn",
    )(block_e, block_src, block_nv, n_used, gather_row, scatter_row, h8, w_gate, w_up, w_down)


def _combine_body(h8_ref, y0_ref, y1_ref, route_ref, g_ref, b_ref, o_ref, *, tt, alpha):
    route = route_ref[...]
    w0 = route[:, 2:3]
    w1 = route[:, 3:4]
    h = _load_token_tiles(h8_ref, tt)
    moe = w0 * _load_token_tiles(y0_ref, tt) + w1 * _load_token_tiles(y1_ref, tt)
    o_ref[...] = _layernorm(alpha * h + moe, g_ref[...], b_ref[...])


def _combine(h8, out2, route, ln_g, ln_b, *, alpha, tt):
    t = route.shape[0]
    d = SUBLANES * LANES
    n_tiles = t // tt
    tile_rows = pl.BlockSpec((tt * SUBLANES, LANES), lambda i: (i, 0))
    return pl.pallas_call(
        functools.partial(_combine_body, tt=tt, alpha=alpha),
        grid=(n_tiles,),
        in_specs=[
            tile_rows,
            tile_rows,
            pl.BlockSpec((tt * SUBLANES, LANES), lambda i: (n_tiles + i, 0)),
            pl.BlockSpec((tt, LANES), lambda i: (i, 0)),
            pl.BlockSpec((1, d), lambda i: (0, 0)),
            pl.BlockSpec((1, d), lambda i: (0, 0)),
        ],
        out_specs=pl.BlockSpec((tt, d), lambda i: (i, 0)),
        out_shape=jax.ShapeDtypeStruct((t, d), F32),
        compiler_params=pltpu.CompilerParams(
            dimension_semantics=("parallel",), vmem_limit_bytes=VMEM_LIMIT),
        name="moe_combine",
    )(h8, out2, out2, route, ln_g, ln_b)


def _moe_layout(route, *, bm):
    t = route.shape[0]
    tk = 2 * t
    flat_e = route[:, 0:2].astype(I32).reshape(-1)
    key = jnp.sort(flat_e * tk + jnp.arange(tk, dtype=I32))
    order = key % tk
    experts = jnp.arange(N_EXPERTS, dtype=I32)
    counts = jnp.sum((experts[:, None] == flat_e[None, :]).astype(I32), axis=1)
    starts = jnp.cumsum(counts) - counts
    pcounts = (counts + bm - 1) // bm * bm
    pends = jnp.cumsum(pcounts)
    pstarts = pends - pcounts
    n_blocks = (tk + N_EXPERTS * bm) // bm
    row0 = jnp.arange(n_blocks, dtype=I32) * bm
    block_e = jnp.minimum(
        jnp.sum((pends[None, :] <= row0[:, None]).astype(I32), axis=1), N_EXPERTS - 1)
    off = row0 - pstarts[block_e]
    block_src = starts[block_e] + off
    block_nv = jnp.clip(counts[block_e] - off, 0, bm)
    n_used = (pends[-1] // bm).astype(I32).reshape(1)
    spare = jnp.zeros((bm,), I32)
    gather_row = jnp.concatenate([(order // 2) * SUBLANES, spare])
    scatter_row = jnp.concatenate([((order % 2) * t + order // 2) * SUBLANES, spare])
    return (block_e, block_src.astype(I32), block_nv.astype(I32), n_used, gather_row.astype(I32),
            scatter_row.astype(I32))


def _layer(x, w_in, lam_q1, lam_k1, lam_q2, lam_k2, subln_g, a_re, a_im, log_dt, b_re, b_im,
           c_re, c_im, d, w_glu, b_glu, w_out, ln1_g, ln1_b, w_rg, w_re, w_gate, w_up, w_down,
           ln2_g, ln2_b, *, layer_idx, alpha, ts, qb, tb, bm, tt):
    b, s, dm = x.shape
    lam_init = 0.8 - 0.6 * math.exp(-0.3 * layer_idx)

    q, k, v, u = _in_proj(x, w_in.astype(BF16), ts=ts)

    lam_params = jnp.stack([lam_q1, lam_k1, lam_q2, lam_k2]).astype(F32)
    a_out = _diff_attention(q, k, v, lam_params, subln_g.reshape(1, -1).astype(F32),
                            lam_init=lam_init, qb=qb)

    bbd, cbd, lre, lim = _ssm_params(a_re, a_im, log_dt, b_re, b_im, c_re, c_im)
    s_out = _ssm_glu(u, bbd, cbd, lre, lim, d.reshape(1, -1).astype(F32), w_glu.astype(BF16),
                     b_glu.reshape(1, -1).astype(F32), tb=tb)

    w_r = jnp.concatenate([w_rg, w_re], axis=1).astype(F32)
    w_r_hi = w_r.astype(BF16)
    w_r_lo = (w_r - w_r_hi.astype(F32)).astype(BF16)
    w_route = jnp.concatenate(
        [w_r_hi, w_r_lo, jnp.zeros((dm, LANES - 2 * N_ROUTE_COLS), BF16)], axis=1)
    h8, route = _out_proj(a_out, s_out, x, w_out.astype(BF16), ln1_g.reshape(1, -1),
                          ln1_b.reshape(1, -1), w_route, alpha=alpha, ts=ts)

    route = route.reshape(b * s, LANES)
    out2 = _moe_ffn(*_moe_layout(route, bm=bm), h8, w_gate, w_up, w_down, bm=bm)
    out = _combine(h8, out2, route, ln2_g.reshape(1, -1), ln2_b.reshape(1, -1), alpha=alpha, tt=tt)
    return out.reshape(b, s, dm)


def kernel(x, w_in, lam_q1, lam_k1, lam_q2, lam_k2, subln_g, ssm_a_re, ssm_a_im, ssm_log_dt, ssm_b_re, ssm_b_im, ssm_c_re, ssm_c_im, ssm_d, w_glu, b_glu, w_out, ln1_g, ln1_b, w_router_group, w_router_expert, w_exp_gate, w_exp_up, w_exp_down, ln2_g, ln2_b):
    depth = w_in.shape[0]
    alpha = (2.0 * depth) ** 0.25
    for i in range(depth):
        x = _layer(
            x, w_in[i], lam_q1[i], lam_k1[i], lam_q2[i], lam_k2[i], subln_g[i], ssm_a_re[i],
            ssm_a_im[i], ssm_log_dt[i], ssm_b_re[i], ssm_b_im[i], ssm_c_re[i], ssm_c_im[i],
            ssm_d[i], w_glu[i], b_glu[i], w_out[i], ln1_g[i], ln1_b[i], w_router_group[i],
            w_router_expert[i], w_exp_gate[i], w_exp_up[i], w_exp_down[i], ln2_g[i], ln2_b[i],
            layer_idx=i, alpha=alpha, ts=512, qb=256, tb=256, bm=256, tt=512)
    return x
```

```python
import functools
import math

import jax
import jax.numpy as jnp
from jax import lax
from jax.experimental import pallas as pl
from jax.experimental.pallas import tpu as pltpu

F32 = jnp.float32
BF16 = jnp.bfloat16
I32 = jnp.int32

DEPTH = 1
N_HEADS = 4
HEAD_DIM = 64
ATTN_WIDTH = N_HEADS * 2 * HEAD_DIM
CHUNK = 64
GROUP_CH = 16
SSM_STATE = 64
N_EXPERT_GROUPS = 4
EXPERTS_PER_GROUP = 8
N_EXPERTS = N_EXPERT_GROUPS * EXPERTS_PER_GROUP
LN_EPS = 1e-5
RMS_EPS = 1e-5

LANES = 128
SUBLANES = 8
VMEM_LIMIT = 56 * 1024 * 1024

NEG_BIG = -1e30


def _in_proj_body(x_ref, w_ref, q_ref, k_ref, v_ref, u_ref, *, width, q_scale):
    x = x_ref[0].astype(BF16)

    def proj(c):
        return jnp.dot(x, w_ref[:, c * width:(c + 1) * width], preferred_element_type=F32)

    q_ref[0] = (proj(0) * q_scale).astype(BF16)
    k_ref[0] = proj(1).astype(BF16)
    v_ref[0] = proj(2).astype(BF16)
    u_ref[0] = proj(3)


def _in_proj(x, w_in_bf16, *, ts):
    b, s, d = x.shape
    width = ATTN_WIDTH
    out_block = pl.BlockSpec((1, ts, width), lambda bi, i: (bi, i, 0))
    return pl.pallas_call(
        functools.partial(_in_proj_body, width=width, q_scale=HEAD_DIM ** -0.5 * math.log2(math.e)),
        grid=(b, s // ts),
        in_specs=[
            pl.BlockSpec((1, ts, d), lambda bi, i: (bi, i, 0)),
            pl.BlockSpec((d, 4 * width), lambda bi, i: (0, 0)),
        ],
        out_specs=[out_block, out_block, out_block, out_block],
        out_shape=[
            jax.ShapeDtypeStruct((b, s, width), BF16),
            jax.ShapeDtypeStruct((b, s, width), BF16),
            jax.ShapeDtypeStruct((b, s, width), BF16),
            jax.ShapeDtypeStruct((b, s, width), F32),
        ],
        compiler_params=pltpu.CompilerParams(
            dimension_semantics=("parallel", "parallel"), vmem_limit_bytes=VMEM_LIMIT),
        name="in_proj",
    )(x, w_in_bf16)


ATTN_ROWS = 16
ATTN_TILES_PER_STEP = 8


def _attn_body(lamp_ref, g_ref, q_ref, k_ref, v_ref, o_ref,
               qq_scr, m_scr, l_scr, acc_scr, *, lam_init, n_qt, qb, kb):
    lp = lamp_ref[...]
    s1 = jnp.sum(lp[0:1] * lp[1:2], axis=-1, keepdims=True)
    s2 = jnp.sum(lp[2:3] * lp[3:4], axis=-1, keepdims=True)
    lam = jnp.exp(s1) - jnp.exp(s2) + lam_init

    lane = lax.broadcasted_iota(I32, (qb, 2 * HEAD_DIM), 1)
    n_sub = 2 * qb // ATTN_ROWS

    def kv_step(start, width, diag_tile=False):
        start = pl.multiple_of(start, qb)
        kblk = k_ref[0, pl.ds(start, width), :]
        s = lax.dot_general(
            qq_scr[...], kblk, (((1,), (1,)), ((), ())), preferred_element_type=F32)
        if diag_tile:
            q_row = lax.broadcasted_iota(I32, (2 * qb, qb), 0) % qb
            k_col = lax.broadcasted_iota(I32, (2 * qb, qb), 1)
            diag = jnp.where(k_col // CHUNK <= q_row // CHUNK, s[:, width - qb:], NEG_BIG)
            s = diag if width == qb else jnp.concatenate([s[:, :width - qb], diag], axis=1)
        m_old = m_scr[...]
        m_new = jnp.maximum(m_old, jnp.max(s, axis=-1, keepdims=True))
        a = jnp.exp2(m_old - m_new)
        m_scr[...] = m_new
        l_parts, p_parts = [], []
        for c in range(n_sub):
            rows = slice(c * ATTN_ROWS, (c + 1) * ATTN_ROWS)
            m_c = m_new[rows]
            cols = [jnp.exp2(s[rows, j * LANES:(j + 1) * LANES] - m_c) for j in range(width // LANES)]
            l_parts.append(functools.reduce(lambda x, y: x + y, cols))
            p_parts.append(jnp.concatenate(cols, axis=1).astype(BF16))
        l_scr[...] = a * l_scr[...] + jnp.concatenate(l_parts, axis=0)
        vblk = v_ref[0, pl.ds(start, width), :]
        pv = jnp.dot(jnp.concatenate(p_parts, axis=0), vblk, preferred_element_type=F32)
        acc_scr[...] = a * acc_scr[...] + pv

    def q_tile(i, carry_unused):
        q = q_ref[0, pl.ds(pl.multiple_of(i * qb, qb), qb), :]
        zero = jnp.zeros_like(q)
        qq_scr[0:qb, :] = jnp.where(lane < HEAD_DIM, q, zero)
        qq_scr[qb:2 * qb, :] = jnp.where(lane >= HEAD_DIM, q, zero)
        m_scr[...] = jnp.full(m_scr.shape, NEG_BIG, F32)
        l_scr[...] = jnp.zeros(l_scr.shape, F32)
        acc_scr[...] = jnp.zeros(acc_scr.shape, F32)

        tiles_per_block = kb // qb

        def full_block(j, c):
            kv_step(j * kb, kb)
            return c

        lax.fori_loop(0, i // tiles_per_block, full_block, 0)

        for r in range(tiles_per_block):
            @pl.when(i % tiles_per_block == r)
            def _(r=r):
                kv_step((i - r) * qb, (r + 1) * qb, diag_tile=True)

        o = acc_scr[...] / jnp.sum(l_scr[...], axis=-1, keepdims=True)
        od = o[:qb] - lam * o[qb:]
        ms = jnp.mean(od * od, axis=-1, keepdims=True)
        od = od * lax.rsqrt(ms + RMS_EPS) * g_ref[...] * (1.0 - lam_init)
        o_ref[0, pl.ds(pl.multiple_of(i * qb, qb), qb), :] = od.astype(BF16)
        return carry_unused

    lax.fori_loop(0, n_qt, q_tile, 0)


def _diff_attention(q, k, v, lam_params, subln_g, *, lam_init, qb):
    b, s, width = q.shape
    hw = 2 * HEAD_DIM
    kb = ATTN_TILES_PER_STEP * qb
    seq_block = pl.BlockSpec((1, s, hw), lambda bi, h: (bi, 0, h))
    return pl.pallas_call(
        functools.partial(_attn_body, lam_init=lam_init, n_qt=s // qb, qb=qb, kb=kb),
        grid=(b, N_HEADS),
        in_specs=[
            pl.BlockSpec((4, HEAD_DIM), lambda bi, h: (0, 0)),
            pl.BlockSpec((1, hw), lambda bi, h: (0, 0)),
            seq_block, seq_block, seq_block,
        ],
        out_specs=seq_block,
        out_shape=jax.ShapeDtypeStruct((b, s, width), BF16),
        scratch_shapes=[
            pltpu.VMEM((2 * qb, hw), BF16),
            pltpu.VMEM((2 * qb, LANES), F32),
            pltpu.VMEM((2 * qb, LANES), F32),
            pltpu.VMEM((2 * qb, hw), F32),
        ],
        compiler_params=pltpu.CompilerParams(
            dimension_semantics=("parallel", "parallel"), vmem_limit_bytes=VMEM_LIMIT),
        name="diff_attn",
    )(lam_params, subln_g, q, k, v)


SCAN_UNROLL = 8


def _ssm_body(u_ref, bbd_ref, cbd_ref, lre_ref, lim_ref, d_ref, wg_ref, bg_ref, z_ref,
              scr, hst, ysc, *, tb, tbp, n_lt):
    nb = u_ref.shape[0]
    n_half = 4

    @pl.when(pl.program_id(0) == 0)
    def _():
        hst[...] = jnp.zeros_like(hst)

    for lt in range(n_lt):
        lanes = slice(lt * LANES, (lt + 1) * LANES)
        for b in range(nb):
            ub = u_ref[b, :, lanes].astype(BF16)
            bu = jnp.dot(ub, bbd_ref[lt], preferred_element_type=F32)
            for k in range(2 * n_half):
                scr[k, b * tbp:b * tbp + tb, :] = bu[:, k * LANES:(k + 1) * LANES]

        lr = [jnp.broadcast_to(lre_ref[lt, :, k * LANES:(k + 1) * LANES], (nb, LANES))
              for k in range(n_half)]
        li = [jnp.broadcast_to(lim_ref[lt, :, k * LANES:(k + 1) * LANES], (nb, LANES))
              for k in range(n_half)]

        def step(t, h):
            new_re, new_im = [], []
            for k in range(n_half):
                rows = pl.ds(t, nb, stride=tbp)
                hr, hi = h[k], h[n_half + k]
                nr = lr[k] * hr - li[k] * hi + scr[k, rows, :]
                ni = lr[k] * hi + li[k] * hr + scr[n_half + k, rows, :]
                scr[k, rows, :] = nr
                scr[n_half + k, rows, :] = ni
                new_re.append(nr)
                new_im.append(ni)
            return tuple(new_re + new_im)

        h = lax.fori_loop(0, tb, step, tuple(hst[lt, k] for k in range(2 * n_half)),
                          unroll=SCAN_UNROLL)
        for k in range(2 * n_half):
            hst[lt, k] = h[k]

        for b in range(nb):
            hb = jnp.concatenate(
                [scr[k, b * tbp:b * tbp + tb, :] for k in range(2 * n_half)], axis=1)
            yb = jnp.dot(hb.astype(BF16), cbd_ref[lt], preferred_element_type=F32)
            ysc[b, :, lanes] = yb + d_ref[:, lanes] * u_ref[b, :, lanes]

    for b in range(nb):
        z = jax.nn.gelu(ysc[b])
        gate = jax.nn.sigmoid(
            jnp.dot(z.astype(BF16), wg_ref[...], preferred_element_type=F32) + bg_ref[...])
        z_ref[b] = (z * gate).astype(BF16)


def _ssm_params(a_re, a_im, log_dt, b_re, b_im, c_re, c_im):
    g = a_re.shape[0]
    n_lt = g * GROUP_CH // LANES
    gpt = g // n_lt
    dt = jnp.exp(log_dt.astype(F32))[:, None]
    ar, ai = a_re.astype(F32), a_im.astype(F32)
    mag = jnp.exp(ar * dt)
    lam_re = mag * jnp.cos(ai * dt)
    lam_im = mag * jnp.sin(ai * dt)
    nr, ni = lam_re - 1.0, lam_im
    den = ar * ar + ai * ai
    cr = (nr * ar + ni * ai) / den
    ci = (ni * ar - nr * ai) / den
    bb_re = cr[..., None] * b_re - ci[..., None] * b_im
    bb_im = cr[..., None] * b_im + ci[..., None] * b_re
    eye = jnp.eye(gpt, dtype=F32)

    def pack_b(bb):
        bb = bb.reshape(n_lt, gpt, SSM_STATE, GROUP_CH)
        m = jnp.einsum('lgpc,gh->lgchp', bb, eye)
        return m.reshape(n_lt, gpt * GROUP_CH, gpt * SSM_STATE)

    def pack_c(cc):
        cc = cc.astype(F32).reshape(n_lt, gpt, GROUP_CH, SSM_STATE)
        m = jnp.einsum('lgcp,gh->lgphc', cc, eye)
        return m.reshape(n_lt, gpt * SSM_STATE, gpt * GROUP_CH)

    bbd = jnp.concatenate([pack_b(bb_re), pack_b(bb_im)], axis=2).astype(BF16)
    cbd = jnp.concatenate([pack_c(c_re), -pack_c(c_im)], axis=1).astype(BF16)
    lre = lam_re.reshape(n_lt, 1, gpt * SSM_STATE)
    lim = lam_im.reshape(n_lt, 1, gpt * SSM_STATE)
    return bbd, cbd, lre, lim


def _ssm_glu(u, bbd, cbd, lre, lim, d, w_glu_bf16, b_glu, *, tb):
    b, s, width = u.shape
    assert b == SUBLANES
    n_lt = width // LANES
    tbp = tb + SUBLANES
    full3 = lambda a: pl.BlockSpec(a.shape, lambda i: (0, 0, 0))
    full2 = lambda a: pl.BlockSpec(a.shape, lambda i: (0, 0))
    return pl.pallas_call(
        functools.partial(_ssm_body, tb=tb, tbp=tbp, n_lt=n_lt),
        grid=(s // tb,),
        in_specs=[
            pl.BlockSpec((b, tb, width), lambda i: (0, i, 0)),
            full3(bbd), full3(cbd), full3(lre), full3(lim), full2(d), full2(w_glu_bf16),
            full2(b_glu),
        ],
        out_specs=pl.BlockSpec((b, tb, width), lambda i: (0, i, 0)),
        out_shape=jax.ShapeDtypeStruct((b, s, width), BF16),
        scratch_shapes=[
            pltpu.VMEM((8, b * tbp, LANES), F32),
            pltpu.VMEM((n_lt, 8, b, LANES), F32),
            pltpu.VMEM((b, tb, width), F32),
        ],
        compiler_params=pltpu.CompilerParams(
            dimension_semantics=("arbitrary",), vmem_limit_bytes=VMEM_LIMIT),
        name="ssm_glu",
    )(u, bbd, cbd, lre, lim, d, w_glu_bf16, b_glu)


def _layernorm(y, g, b):
    mu = jnp.mean(y, axis=-1, keepdims=True)
    yc = y - mu
    var = jnp.mean(yc * yc, axis=-1, keepdims=True)
    return yc * lax.rsqrt(var + LN_EPS) * g + b


N_ROUTE_COLS = N_EXPERT_GROUPS + N_EXPERTS


def _store_token_tiles(ref, val):
    n = val.shape[0]
    for j in range(SUBLANES):
        ref[pl.ds(j, n, stride=SUBLANES), :] = val[:, j * LANES:(j + 1) * LANES]


def _load_token_tiles(ref, n):
    return jnp.concatenate(
        [ref[pl.ds(j, n, stride=SUBLANES), :] for j in range(SUBLANES)], axis=1)


OUT_PROJ_SPLITS = 2


def _out_proj_body(a_ref, z_ref, x_ref, wo_ref, g_ref, b_ref, wr_ref, h8_ref, route_ref, *, alpha):
    n = a_ref.shape[1] // OUT_PROJ_SPLITS
    for part in range(OUT_PROJ_SPLITS):
        rows = pl.ds(part * n, n)
        _out_proj_rows(a_ref[0, rows, :], z_ref[0, rows, :], x_ref[0, rows, :], wo_ref, g_ref,
                       b_ref, wr_ref, h8_ref.at[pl.ds(part * n * SUBLANES, n * SUBLANES)],
                       route_ref.at[0, rows], alpha=alpha)


def _out_proj_rows(a, z, x, wo_ref, g_ref, b_ref, wr_ref, h8_ref, route_ref, *, alpha):
    aw = a.shape[1]
    mix = jnp.dot(a, wo_ref[0:aw, :], preferred_element_type=F32)
    mix += jnp.dot(z, wo_ref[aw:, :], preferred_element_type=F32)
    h = _layernorm(alpha * x + mix, g_ref[...], b_ref[...])
    _store_token_tiles(h8_ref, h)

    h_hi = h.astype(BF16)
    h_lo = (h - h_hi.astype(F32)).astype(BF16)
    lg = jnp.dot(h_hi, wr_ref[...], preferred_element_type=F32)
    lg += jnp.dot(h_lo, wr_ref[...], preferred_element_type=F32)
    lg = lg + pltpu.roll(lg, LANES - N_ROUTE_COLS, 1)

    lane = lax.broadcasted_iota(I32, lg.shape, 1)

    def argmax_lane(vals):
        mx = jnp.max(vals, axis=-1, keepdims=True)
        idx = jnp.min(jnp.where(vals == mx, lane, LANES), axis=-1, keepdims=True)
        return mx, idx

    gl = jnp.where(lane < N_EXPERT_GROUPS, lg, NEG_BIG)
    g_max, g_idx = argmax_lane(gl)
    g_p = 1.0 / jnp.sum(jnp.exp(gl - g_max), axis=-1, keepdims=True)

    e_lo = N_EXPERT_GROUPS + g_idx * EXPERTS_PER_GROUP
    el = jnp.where((lane >= e_lo) & (lane < e_lo + EXPERTS_PER_GROUP), lg, NEG_BIG)
    e1, i1 = argmax_lane(el)
    e2, i2 = argmax_lane(jnp.where(lane == i1, NEG_BIG, el))
    r = jnp.exp(e2 - e1)
    w1 = g_p / (1.0 + r)
    w2 = g_p * r / (1.0 + r)
    route = jnp.where(lane == 0, (i1 - N_EXPERT_GROUPS).astype(F32), 0.0)
    route = jnp.where(lane == 1, (i2 - N_EXPERT_GROUPS).astype(F32), route)
    route = jnp.where(lane == 2, w1, route)
    route = jnp.where(lane == 3, w2, route)
    route_ref[...] = route


def _out_proj(a, z, x, w_out_bf16, ln_g, ln_b, w_route, *, alpha, ts):
    b, s, d = x.shape
    aw = a.shape[2]
    half = pl.BlockSpec((1, ts, aw), lambda bi, i: (bi, i, 0))
    row = pl.BlockSpec((1, ts, d), lambda bi, i: (bi, i, 0))
    full2 = lambda arr: pl.BlockSpec(arr.shape, lambda bi, i: (0, 0))
    assert d == SUBLANES * LANES
    n_st = s // ts
    return pl.pallas_call(
        functools.partial(_out_proj_body, alpha=alpha),
        grid=(b, n_st),
        in_specs=[half, half, row, full2(w_out_bf16), full2(ln_g), full2(ln_b), full2(w_route)],
        out_specs=[
            pl.BlockSpec((ts * SUBLANES, LANES), lambda bi, i: (bi * n_st + i, 0)),
            pl.BlockSpec((1, ts, LANES), lambda bi, i: (bi, i, 0)),
        ],
        out_shape=[
            jax.ShapeDtypeStruct((b * s * SUBLANES, LANES), F32),
            jax.ShapeDtypeStruct((b, s, LANES), F32),
        ],
        compiler_params=pltpu.CompilerParams(
            dimension_semantics=("parallel", "parallel"), vmem_limit_bytes=VMEM_LIMIT),
        name="out_proj_router",
    )(a, z, x, w_out_bf16, ln_g, ln_b, w_route)


ISSUE_UNROLL = 8


def _moe_body(be_ref, src_ref, nv_ref, nused_ref, grow_ref, srow_ref, h8_hbm, wg_ref, wu_ref,
              wd_ref, out_hbm, xbuf, ybuf, wgb, wub, wdb, gsem, ssem, *, bm, n_tok):
    i = pl.program_id(0)
    n_used = nused_ref[0]
    n_pairs = 2 * n_tok
    tile = SUBLANES

    def tile_row(r):
        return r * tile if isinstance(r, int) else pl.multiple_of(r * tile, tile)

    def gather_copy(blk, buf_slot, r):
        src_row = grow_ref[src_ref[blk] + r]
        return pltpu.make_async_copy(
            h8_hbm.at[pl.ds(pl.multiple_of(src_row, tile), tile)],
            xbuf.at[buf_slot, pl.ds(tile_row(r), tile)],
            gsem.at[buf_slot])

    def scatter_copy(blk, buf_slot, r):
        blk_c = jnp.maximum(blk, 0)
        n_valid = jnp.where(blk >= 0, nv_ref[blk_c], 0)
        real_row = srow_ref[src_ref[blk_c] + r]
        pad_row = (n_pairs + buf_slot * bm + r) * tile
        dst_row = jnp.where(r < n_valid, real_row, pad_row)
        return pltpu.make_async_copy(
            ybuf.at[buf_slot, pl.ds(tile_row(r), tile)],
            out_hbm.at[pl.ds(pl.multiple_of(dst_row, tile), tile)],
            ssem.at[buf_slot])

    def issue_loop(make_copy, blk, buf_slot):
        def trip(g, c):
            for q in range(ISSUE_UNROLL):
                make_copy(blk, buf_slot, pl.multiple_of(g * ISSUE_UNROLL, ISSUE_UNROLL) + q).start()
            return c
        lax.fori_loop(0, bm // ISSUE_UNROLL, trip, 0)

    def wait_gather(buf_slot):
        pltpu.make_async_copy(h8_hbm.at[pl.ds(0, bm * tile)], xbuf.at[buf_slot],
                              gsem.at[buf_slot]).wait()

    def wait_scatter(buf_slot):
        pltpu.make_async_copy(ybuf.at[buf_slot], out_hbm.at[pl.ds(0, bm * tile)],
                              ssem.at[buf_slot]).wait()

    @pl.when(i == 0)
    def _():
        issue_loop(gather_copy, 0, 0)
        ybuf[...] = jnp.zeros_like(ybuf)
        for s in range(2):
            pad_rows = out_hbm.at[pl.ds((n_pairs + s * bm) * tile, bm * tile)]
            pltpu.make_async_copy(ybuf.at[s], pad_rows, ssem.at[s]).start()
            pltpu.make_async_copy(ybuf.at[s], pad_rows, ssem.at[s]).wait()

    def block_step(slot):
        other = 1 - slot
        wait_gather(slot)

        @pl.when(i >= 1)
        def _():
            wait_scatter(slot)

        prev = be_ref[jnp.maximum(i - 1, 0)]

        @pl.when((i == 0) | (be_ref[i] != prev))
        def _():
            wgb[...] = wg_ref[0].astype(BF16)
            wub[...] = wu_ref[0].astype(BF16)
            wdb[...] = wd_ref[0].astype(BF16)

        x = _load_token_tiles(xbuf.at[slot], bm).astype(BF16)
        nxt = jnp.minimum(i + 1, n_used - 1)
        for r in range(bm):
            gather_copy(nxt, other, r).start()
            scatter_copy(i - 1, other, r).start()
        gate = jnp.dot(x, wgb[...], preferred_element_type=F32)
        up = jnp.dot(x, wub[...], preferred_element_type=F32)
        hid = (jax.nn.silu(gate) * up).astype(BF16)
        y = jnp.dot(hid, wdb[...], preferred_element_type=F32)
        _store_token_tiles(ybuf.at[slot], y)

    for s in range(2):
        pl.when((i < n_used) & (i % 2 == s))(functools.partial(block_step, s))

    @pl.when(i == n_used)
    def _():
        last_slot = (n_used - 1) % 2
        issue_loop(scatter_copy, n_used - 1, last_slot)
        wait_scatter(0)
        wait_scatter(1)
        pltpu.make_async_copy(h8_hbm.at[pl.ds(0, bm * tile)], xbuf.at[1 - last_slot],
                              gsem.at[1 - last_slot]).wait()


def _moe_ffn(block_e, block_src, block_nv, n_used, gather_row, scatter_row, h8, w_gate, w_up,
             w_down, *, bm):
    n_tok = h8.shape[0] // SUBLANES
    d = SUBLANES * LANES
    n_blocks = block_e.shape[0]
    de = w_gate.shape[2]
    out_rows = (2 * n_tok + 2 * bm) * SUBLANES
    wspec = lambda shape: pl.BlockSpec(shape, lambda i, be, *_: (be[i], 0, 0))
    grid_spec = pltpu.PrefetchScalarGridSpec(
        num_scalar_prefetch=6,
        grid=(n_blocks,),
        in_specs=[
            pl.BlockSpec(memory_space=pl.ANY),
            wspec((1, d, de)), wspec((1, d, de)), wspec((1, de, d)),
        ],
        out_specs=pl.BlockSpec(memory_space=pl.ANY),
        scratch_shapes=[
            pltpu.VMEM((2, bm * SUBLANES, LANES), F32),
            pltpu.VMEM((2, bm * SUBLANES, LANES), F32),
            pltpu.VMEM((d, de), BF16),
            pltpu.VMEM((d, de), BF16),
            pltpu.VMEM((de, d), BF16),
            pltpu.SemaphoreType.DMA((2,)),
            pltpu.SemaphoreType.DMA((2,)),
        ],
    )
    return pl.pallas_call(
        functools.partial(_moe_body, bm=bm, n_tok=n_tok),
        grid_spec=grid_spec,
        out_shape=jax.ShapeDtypeStruct((out_rows, LANES), F32),
        compiler_params=pltpu.CompilerParams(
            dimension_semantics=("arbitrary",), vmem_limit_bytes=VMEM_LIMIT),
        name="moe_ffn",
    )(block_e, block_src, block_nv, n_used, gather_row, scatter_row, h8, w_gate, w_up, w_down)


def _combine_body(h8_ref, y0_ref, y1_ref, route_ref, g_ref, b_ref, o_ref, *, tt, alpha):
    route = route_ref[...]
    w0 = route[:, 2:3]
    w1 = route[:, 3:4]
    h = _load_token_tiles(h8_ref, tt)
    moe = w0 * _load_token_tiles(y0_ref, tt) + w1 * _load_token_tiles(y1_ref, tt)
    o_ref[...] = _layernorm(alpha * h + moe, g_ref[...], b_ref[...])


def _combine(h8, out2, route, ln_g, ln_b, *, alpha, tt):
    t = route.shape[0]
    d = SUBLANES * LANES
    n_tiles = t // tt
    tile_rows = pl.BlockSpec((tt * SUBLANES, LANES), lambda i: (i, 0))
    return pl.pallas_call(
        functools.partial(_combine_body, tt=tt, alpha=alpha),
        grid=(n_tiles,),
        in_specs=[
            tile_rows,
            tile_rows,
            pl.BlockSpec((tt * SUBLANES, LANES), lambda i: (n_tiles + i, 0)),
            pl.BlockSpec((tt, LANES), lambda i: (i, 0)),
            pl.BlockSpec((1, d), lambda i: (0, 0)),
            pl.BlockSpec((1, d), lambda i: (0, 0)),
        ],
        out_specs=pl.BlockSpec((tt, d), lambda i: (i, 0)),
        out_shape=jax.ShapeDtypeStruct((t, d), F32),
        compiler_params=pltpu.CompilerParams(
            dimension_semantics=("parallel",), vmem_limit_bytes=VMEM_LIMIT),
        name="moe_combine",
    )(h8, out2, out2, route, ln_g, ln_b)


def _moe_layout(route, *, bm):
    t = route.shape[0]
    tk = 2 * t
    flat_e = route[:, 0:2].astype(I32).reshape(-1)
    key = jnp.sort(flat_e * tk + jnp.arange(tk, dtype=I32))
    order = key % tk
    experts = jnp.arange(N_EXPERTS, dtype=I32)
    counts = jnp.sum((experts[:, None] == flat_e[None, :]).astype(I32), axis=1)
    starts = jnp.cumsum(counts) - counts
    pcounts = (counts + bm - 1) // bm * bm
    pends = jnp.cumsum(pcounts)
    pstarts = pends - pcounts
    n_blocks = (tk + N_EXPERTS * bm) // bm
    row0 = jnp.arange(n_blocks, dtype=I32) * bm
    block_e = jnp.minimum(
        jnp.sum((pends[None, :] <= row0[:, None]).astype(I32), axis=1), N_EXPERTS - 1)
    pick = (block_e[:, None] == experts[None, :]).astype(I32)
    lookup = lambda table: jnp.sum(pick * table[None, :], axis=1)
    off = row0 - lookup(pstarts)
    block_src = lookup(starts) + off
    block_nv = jnp.clip(lookup(counts) - off, 0, bm)
    n_used = (pends[-1] // bm).astype(I32).reshape(1)
    spare = jnp.zeros((bm,), I32)
    gather_row = jnp.concatenate([(order // 2) * SUBLANES, spare])
    scatter_row = jnp.concatenate([((order % 2) * t + order // 2) * SUBLANES, spare])
    return (block_e, block_src.astype(I32), block_nv.astype(I32), n_used, gather_row.astype(I32),
            scatter_row.astype(I32))


def _layer(x, w_in, lam_q1, lam_k1, lam_q2, lam_k2, subln_g, a_re, a_im, log_dt, b_re, b_im,
           c_re, c_im, d, w_glu, b_glu, w_out, ln1_g, ln1_b, w_rg, w_re, w_gate, w_up, w_down,
           ln2_g, ln2_b, *, layer_idx, alpha, ts, qb, tb, bm, tt):
    b, s, dm = x.shape
    lam_init = 0.8 - 0.6 * math.exp(-0.3 * layer_idx)

    q, k, v, u = _in_proj(x, w_in.astype(BF16), ts=ts)

    lam_params = jnp.stack([lam_q1, lam_k1, lam_q2, lam_k2]).astype(F32)
    a_out = _diff_attention(q, k, v, lam_params, subln_g.reshape(1, -1).astype(F32),
                            lam_init=lam_init, qb=qb)

    bbd, cbd, lre, lim = _ssm_params(a_re, a_im, log_dt, b_re, b_im, c_re, c_im)
    s_out = _ssm_glu(u, bbd, cbd, lre, lim, d.reshape(1, -1).astype(F32), w_glu.astype(BF16),
                     b_glu.reshape(1, -1).astype(F32), tb=tb)

    w_r = jnp.concatenate([w_rg, w_re], axis=1).astype(F32)
    w_r_hi = w_r.astype(BF16)
    w_r_lo = (w_r - w_r_hi.astype(F32)).astype(BF16)
    w_route = jnp.concatenate(
        [w_r_hi, w_r_lo, jnp.zeros((dm, LANES - 2 * N_ROUTE_COLS), BF16)], axis=1)
    h8, route = _out_proj(a_out, s_out, x, w_out.astype(BF16), ln1_g.reshape(1, -1),
                          ln1_b.reshape(1, -1), w_route, alpha=alpha, ts=ts)

    route = route.reshape(b * s, LANES)
    out2 = _moe_ffn(*_moe_layout(route, bm=bm), h8, w_gate, w_up, w_down, bm=bm)
    out = _combine(h8, out2, route, ln2_g.reshape(1, -1), ln2_b.reshape(1, -1), alpha=alpha, tt=tt)
    return out.reshape(b, s, dm)


def kernel(x, w_in, lam_q1, lam_k1, lam_q2, lam_k2, subln_g, ssm_a_re, ssm_a_im, ssm_log_dt, ssm_b_re, ssm_b_im, ssm_c_re, ssm_c_im, ssm_d, w_glu, b_glu, w_out, ln1_g, ln1_b, w_router_group, w_router_expert, w_exp_gate, w_exp_up, w_exp_down, ln2_g, ln2_b):
    depth = w_in.shape[0]
    alpha = (2.0 * depth) ** 0.25
    for i in range(depth):
        x = _layer(
            x, w_in[i], lam_q1[i], lam_k1[i], lam_q2[i], lam_k2[i], subln_g[i], ssm_a_re[i],
            ssm_a_im[i], ssm_log_dt[i], ssm_b_re[i], ssm_b_im[i], ssm_c_re[i], ssm_c_im[i],
            ssm_d[i], w_glu[i], b_glu[i], w_out[i], ln1_g[i], ln1_b[i], w_router_group[i],
            w_router_expert[i], w_exp_gate[i], w_exp_up[i], w_exp_down[i], ln2_g[i], ln2_b[i],
            layer_idx=i, alpha=alpha, ts=512, qb=256, tb=256, bm=256, tt=512)
    return x
```

```python
import functools
import math

import jax
import jax.numpy as jnp
from jax import lax
from jax.experimental import pallas as pl
from jax.experimental.pallas import tpu as pltpu

F32 = jnp.float32
BF16 = jnp.bfloat16
I32 = jnp.int32

DEPTH = 1
N_HEADS = 4
HEAD_DIM = 64
ATTN_WIDTH = N_HEADS * 2 * HEAD_DIM
CHUNK = 64
GROUP_CH = 16
SSM_STATE = 64
N_EXPERT_GROUPS = 4
EXPERTS_PER_GROUP = 8
N_EXPERTS = N_EXPERT_GROUPS * EXPERTS_PER_GROUP
LN_EPS = 1e-5
RMS_EPS = 1e-5

LANES = 128
SUBLANES = 8
VMEM_LIMIT = 56 * 1024 * 1024

NEG_BIG = -1e30


def _in_proj_body(x_ref, w_ref, q_ref, k_ref, v_ref, u_ref, *, width, q_scale):
    x = x_ref[0].astype(BF16)

    def proj(c):
        return jnp.dot(x, w_ref[:, c * width:(c + 1) * width], preferred_element_type=F32)

    q_ref[0] = (proj(0) * q_scale).astype(BF16)
    k_ref[0] = proj(1).astype(BF16)
    v_ref[0] = proj(2).astype(BF16)
    u_ref[0] = proj(3)


def _in_proj(x, w_in_bf16, *, ts):
    b, s, d = x.shape
    width = ATTN_WIDTH
    out_block = pl.BlockSpec((1, ts, width), lambda bi, i: (bi, i, 0))
    return pl.pallas_call(
        functools.partial(_in_proj_body, width=width, q_scale=HEAD_DIM ** -0.5 * math.log2(math.e)),
        grid=(b, s // ts),
        in_specs=[
            pl.BlockSpec((1, ts, d), lambda bi, i: (bi, i, 0)),
            pl.BlockSpec((d, 4 * width), lambda bi, i: (0, 0)),
        ],
        out_specs=[out_block, out_block, out_block, out_block],
        out_shape=[
            jax.ShapeDtypeStruct((b, s, width), BF16),
            jax.ShapeDtypeStruct((b, s, width), BF16),
            jax.ShapeDtypeStruct((b, s, width), BF16),
            jax.ShapeDtypeStruct((b, s, width), F32),
        ],
        compiler_params=pltpu.CompilerParams(
            dimension_semantics=("parallel", "parallel"), vmem_limit_bytes=VMEM_LIMIT),
        name="in_proj",
    )(x, w_in_bf16)


ATTN_ROWS = 16
ATTN_TILES_PER_STEP = 8


def _attn_body(lamp_ref, g_ref, q_ref, k_ref, v_ref, o_ref,
               qq_scr, m_scr, l_scr, acc_scr, *, lam_init, n_qt, qb, kb):
    lp = lamp_ref[...]
    s1 = jnp.sum(lp[0:1] * lp[1:2], axis=-1, keepdims=True)
    s2 = jnp.sum(lp[2:3] * lp[3:4], axis=-1, keepdims=True)
    lam = jnp.exp(s1) - jnp.exp(s2) + lam_init

    lane = lax.broadcasted_iota(I32, (qb, 2 * HEAD_DIM), 1)
    n_sub = 2 * qb // ATTN_ROWS

    def kv_step(start, width, diag_tile=False):
        start = pl.multiple_of(start, qb)
        kblk = k_ref[0, pl.ds(start, width), :]
        s = lax.dot_general(
            qq_scr[...], kblk, (((1,), (1,)), ((), ())), preferred_element_type=F32)
        if diag_tile:
            q_row = lax.broadcasted_iota(I32, (2 * qb, qb), 0) % qb
            k_col = lax.broadcasted_iota(I32, (2 * qb, qb), 1)
            diag = jnp.where(k_col // CHUNK <= q_row // CHUNK, s[:, width - qb:], NEG_BIG)
            s = diag if width == qb else jnp.concatenate([s[:, :width - qb], diag], axis=1)
        m_old = m_scr[...]
        m_new = jnp.maximum(m_old, jnp.max(s, axis=-1, keepdims=True))
        a = jnp.exp2(m_old - m_new)
        m_scr[...] = m_new
        l_parts, p_parts = [], []
        for c in range(n_sub):
            rows = slice(c * ATTN_ROWS, (c + 1) * ATTN_ROWS)
            m_c = m_new[rows]
            cols = [jnp.exp2(s[rows, j * LANES:(j + 1) * LANES] - m_c) for j in range(width // LANES)]
            l_parts.append(functools.reduce(lambda x, y: x + y, cols))
            p_parts.append(jnp.concatenate(cols, axis=1).astype(BF16))
        l_scr[...] = a * l_scr[...] + jnp.concatenate(l_parts, axis=0)
        vblk = v_ref[0, pl.ds(start, width), :]
        pv = jnp.dot(jnp.concatenate(p_parts, axis=0), vblk, preferred_element_type=F32)
        acc_scr[...] = a * acc_scr[...] + pv

    def q_tile(i, carry_unused):
        q = q_ref[0, pl.ds(pl.multiple_of(i * qb, qb), qb), :]
        zero = jnp.zeros_like(q)
        qq_scr[0:qb, :] = jnp.where(lane < HEAD_DIM, q, zero)
        qq_scr[qb:2 * qb, :] = jnp.where(lane >= HEAD_DIM, q, zero)
        m_scr[...] = jnp.full(m_scr.shape, NEG_BIG, F32)
        l_scr[...] = jnp.zeros(l_scr.shape, F32)
        acc_scr[...] = jnp.zeros(acc_scr.shape, F32)

        tiles_per_block = kb // qb

        def full_block(j, c):
            kv_step(j * kb, kb)
            return c

        lax.fori_loop(0, i // tiles_per_block, full_block, 0)

        for r in range(tiles_per_block):
            @pl.when(i % tiles_per_block == r)
            def _(r=r):
                kv_step((i - r) * qb, (r + 1) * qb, diag_tile=True)

        o = acc_scr[...] / jnp.sum(l_scr[...], axis=-1, keepdims=True)
        od = o[:qb] - lam * o[qb:]
        ms = jnp.mean(od * od, axis=-1, keepdims=True)
        od = od * lax.rsqrt(ms + RMS_EPS) * g_ref[...] * (1.0 - lam_init)
        o_ref[0, pl.ds(pl.multiple_of(i * qb, qb), qb), :] = od.astype(BF16)
        return carry_unused

    lax.fori_loop(0, n_qt, q_tile, 0)


def _diff_attention(q, k, v, lam_params, subln_g, *, lam_init, qb):
    b, s, width = q.shape
    hw = 2 * HEAD_DIM
    kb = ATTN_TILES_PER_STEP * qb
    seq_block = pl.BlockSpec((1, s, hw), lambda bi, h: (bi, 0, h))
    return pl.pallas_call(
        functools.partial(_attn_body, lam_init=lam_init, n_qt=s // qb, qb=qb, kb=kb),
        grid=(b, N_HEADS),
        in_specs=[
            pl.BlockSpec((4, HEAD_DIM), lambda bi, h: (0, 0)),
            pl.BlockSpec((1, hw), lambda bi, h: (0, 0)),
            seq_block, seq_block, seq_block,
        ],
        out_specs=seq_block,
        out_shape=jax.ShapeDtypeStruct((b, s, width), BF16),
        scratch_shapes=[
            pltpu.VMEM((2 * qb, hw), BF16),
            pltpu.VMEM((2 * qb, LANES), F32),
            pltpu.VMEM((2 * qb, LANES), F32),
            pltpu.VMEM((2 * qb, hw), F32),
        ],
        compiler_params=pltpu.CompilerParams(
            dimension_semantics=("parallel", "parallel"), vmem_limit_bytes=VMEM_LIMIT),
        name="diff_attn",
    )(lam_params, subln_g, q, k, v)


SCAN_UNROLL = 8


def _ssm_body(u_ref, bbd_ref, cbd_ref, lre_ref, lim_ref, d_ref, wg_ref, bg_ref, z_ref,
              scr, hst, ysc, *, tb, tbp, n_lt):
    nb = u_ref.shape[0]
    n_half = 4

    @pl.when(pl.program_id(0) == 0)
    def _():
        hst[...] = jnp.zeros_like(hst)

    for lt in range(n_lt):
        lanes = slice(lt * LANES, (lt + 1) * LANES)
        for b in range(nb):
            ub = u_ref[b, :, lanes].astype(BF16)
            bu = jnp.dot(ub, bbd_ref[lt], preferred_element_type=F32)
            for k in range(2 * n_half):
                scr[k, b * tbp:b * tbp + tb, :] = bu[:, k * LANES:(k + 1) * LANES]

        lr = [jnp.broadcast_to(lre_ref[lt, :, k * LANES:(k + 1) * LANES], (nb, LANES))
              for k in range(n_half)]
        li = [jnp.broadcast_to(lim_ref[lt, :, k * LANES:(k + 1) * LANES], (nb, LANES))
              for k in range(n_half)]

        def step(t, h):
            new_re, new_im = [], []
            for k in range(n_half):
                rows = pl.ds(t, nb, stride=tbp)
                hr, hi = h[k], h[n_half + k]
                nr = lr[k] * hr - li[k] * hi + scr[k, rows, :]
                ni = lr[k] * hi + li[k] * hr + scr[n_half + k, rows, :]
                scr[k, rows, :] = nr
                scr[n_half + k, rows, :] = ni
                new_re.append(nr)
                new_im.append(ni)
            return tuple(new_re + new_im)

        h = lax.fori_loop(0, tb, step, tuple(hst[lt, k] for k in range(2 * n_half)),
                          unroll=SCAN_UNROLL)
        for k in range(2 * n_half):
            hst[lt, k] = h[k]

        for b in range(nb):
            hb = jnp.concatenate(
                [scr[k, b * tbp:b * tbp + tb, :] for k in range(2 * n_half)], axis=1)
            yb = jnp.dot(hb.astype(BF16), cbd_ref[lt], preferred_element_type=F32)
            ysc[b, :, lanes] = yb + d_ref[:, lanes] * u_ref[b, :, lanes]

    for b in range(nb):
        z = jax.nn.gelu(ysc[b])
        gate = jax.nn.sigmoid(
            jnp.dot(z.astype(BF16), wg_ref[...], preferred_element_type=F32) + bg_ref[...])
        z_ref[b] = (z * gate).astype(BF16)


def _ssm_params(a_re, a_im, log_dt, b_re, b_im, c_re, c_im):
    g = a_re.shape[0]
    n_lt = g * GROUP_CH // LANES
    gpt = g // n_lt
    dt = jnp.exp(log_dt.astype(F32))[:, None]
    ar, ai = a_re.astype(F32), a_im.astype(F32)
    mag = jnp.exp(ar * dt)
    lam_re = mag * jnp.cos(ai * dt)
    lam_im = mag * jnp.sin(ai * dt)
    nr, ni = lam_re - 1.0, lam_im
    den = ar * ar + ai * ai
    cr = (nr * ar + ni * ai) / den
    ci = (ni * ar - nr * ai) / den
    bb_re = cr[..., None] * b_re - ci[..., None] * b_im
    bb_im = cr[..., None] * b_im + ci[..., None] * b_re
    eye = jnp.eye(gpt, dtype=F32)

    def pack_b(bb):
        bb = bb.reshape(n_lt, gpt, SSM_STATE, GROUP_CH)
        m = jnp.einsum('lgpc,gh->lgchp', bb, eye)
        return m.reshape(n_lt, gpt * GROUP_CH, gpt * SSM_STATE)

    def pack_c(cc):
        cc = cc.astype(F32).reshape(n_lt, gpt, GROUP_CH, SSM_STATE)
        m = jnp.einsum('lgcp,gh->lgphc', cc, eye)
        return m.reshape(n_lt, gpt * SSM_STATE, gpt * GROUP_CH)

    bbd = jnp.concatenate([pack_b(bb_re), pack_b(bb_im)], axis=2).astype(BF16)
    cbd = jnp.concatenate([pack_c(c_re), -pack_c(c_im)], axis=1).astype(BF16)
    lre = lam_re.reshape(n_lt, 1, gpt * SSM_STATE)
    lim = lam_im.reshape(n_lt, 1, gpt * SSM_STATE)
    return bbd, cbd, lre, lim


def _ssm_glu(u, bbd, cbd, lre, lim, d, w_glu_bf16, b_glu, *, tb):
    b, s, width = u.shape
    assert b == SUBLANES
    n_lt = width // LANES
    tbp = tb + SUBLANES
    full3 = lambda a: pl.BlockSpec(a.shape, lambda i: (0, 0, 0))
    full2 = lambda a: pl.BlockSpec(a.shape, lambda i: (0, 0))
    return pl.pallas_call(
        functools.partial(_ssm_body, tb=tb, tbp=tbp, n_lt=n_lt),
        grid=(s // tb,),
        in_specs=[
            pl.BlockSpec((b, tb, width), lambda i: (0, i, 0)),
            full3(bbd), full3(cbd), full3(lre), full3(lim), full2(d), full2(w_glu_bf16),
            full2(b_glu),
        ],
        out_specs=pl.BlockSpec((b, tb, width), lambda i: (0, i, 0)),
        out_shape=jax.ShapeDtypeStruct((b, s, width), BF16),
        scratch_shapes=[
            pltpu.VMEM((8, b * tbp, LANES), F32),
            pltpu.VMEM((n_lt, 8, b, LANES), F32),
            pltpu.VMEM((b, tb, width), F32),
        ],
        compiler_params=pltpu.CompilerParams(
            dimension_semantics=("arbitrary",), vmem_limit_bytes=VMEM_LIMIT),
        name="ssm_glu",
    )(u, bbd, cbd, lre, lim, d, w_glu_bf16, b_glu)


def _layernorm(y, g, b):
    mu = jnp.mean(y, axis=-1, keepdims=True)
    yc = y - mu
    var = jnp.mean(yc * yc, axis=-1, keepdims=True)
    return yc * lax.rsqrt(var + LN_EPS) * g + b


N_ROUTE_COLS = N_EXPERT_GROUPS + N_EXPERTS


def _store_token_tiles(ref, val):
    n = val.shape[0]
    for j in range(SUBLANES):
        ref[pl.ds(j, n, stride=SUBLANES), :] = val[:, j * LANES:(j + 1) * LANES]


def _load_token_tiles(ref, n):
    return jnp.concatenate(
        [ref[pl.ds(j, n, stride=SUBLANES), :] for j in range(SUBLANES)], axis=1)


OUT_PROJ_SPLITS = 2


def _out_proj_body(a_ref, z_ref, x_ref, wo_ref, g_ref, b_ref, wr_ref, h8_ref, route_ref, *, alpha):
    n = a_ref.shape[1] // OUT_PROJ_SPLITS
    for part in range(OUT_PROJ_SPLITS):
        rows = pl.ds(part * n, n)
        _out_proj_rows(a_ref[0, rows, :], z_ref[0, rows, :], x_ref[0, rows, :], wo_ref, g_ref,
                       b_ref, wr_ref, h8_ref.at[pl.ds(part * n * SUBLANES, n * SUBLANES)],
                       route_ref.at[0, rows], alpha=alpha)


def _out_proj_rows(a, z, x, wo_ref, g_ref, b_ref, wr_ref, h8_ref, route_ref, *, alpha):
    aw = a.shape[1]
    mix = jnp.dot(a, wo_ref[0:aw, :], preferred_element_type=F32)
    mix += jnp.dot(z, wo_ref[aw:, :], preferred_element_type=F32)
    h = _layernorm(alpha * x + mix, g_ref[...], b_ref[...])
    _store_token_tiles(h8_ref, h)

    h_hi = h.astype(BF16)
    h_lo = (h - h_hi.astype(F32)).astype(BF16)
    lg = jnp.dot(h_hi, wr_ref[...], preferred_element_type=F32)
    lg += jnp.dot(h_lo, wr_ref[...], preferred_element_type=F32)
    lg = lg + pltpu.roll(lg, LANES - N_ROUTE_COLS, 1)

    lane = lax.broadcasted_iota(I32, lg.shape, 1)

    def argmax_lane(vals):
        mx = jnp.max(vals, axis=-1, keepdims=True)
        idx = jnp.min(jnp.where(vals == mx, lane, LANES), axis=-1, keepdims=True)
        return mx, idx

    gl = jnp.where(lane < N_EXPERT_GROUPS, lg, NEG_BIG)
    g_max, g_idx = argmax_lane(gl)
    g_p = 1.0 / jnp.sum(jnp.exp(gl - g_max), axis=-1, keepdims=True)

    e_lo = N_EXPERT_GROUPS + g_idx * EXPERTS_PER_GROUP
    el = jnp.where((lane >= e_lo) & (lane < e_lo + EXPERTS_PER_GROUP), lg, NEG_BIG)
    e1, i1 = argmax_lane(el)
    e2, i2 = argmax_lane(jnp.where(lane == i1, NEG_BIG, el))
    r = jnp.exp(e2 - e1)
    w1 = g_p / (1.0 + r)
    w2 = g_p * r / (1.0 + r)
    route = jnp.where(lane == 0, (i1 - N_EXPERT_GROUPS).astype(F32), 0.0)
    route = jnp.where(lane == 1, (i2 - N_EXPERT_GROUPS).astype(F32), route)
    route = jnp.where(lane == 2, w1, route)
    route = jnp.where(lane == 3, w2, route)
    route_ref[...] = route


def _out_proj(a, z, x, w_out_bf16, ln_g, ln_b, w_route, *, alpha, ts):
    b, s, d = x.shape
    aw = a.shape[2]
    half = pl.BlockSpec((1, ts, aw), lambda bi, i: (bi, i, 0))
    row = pl.BlockSpec((1, ts, d), lambda bi, i: (bi, i, 0))
    full2 = lambda arr: pl.BlockSpec(arr.shape, lambda bi, i: (0, 0))
    assert d == SUBLANES * LANES
    n_st = s // ts
    return pl.pallas_call(
        functools.partial(_out_proj_body, alpha=alpha),
        grid=(b, n_st),
        in_specs=[half, half, row, full2(w_out_bf16), full2(ln_g), full2(ln_b), full2(w_route)],
        out_specs=[
            pl.BlockSpec((ts * SUBLANES, LANES), lambda bi, i: (bi * n_st + i, 0)),
            pl.BlockSpec((1, ts, LANES), lambda bi, i: (bi, i, 0)),
        ],
        out_shape=[
            jax.ShapeDtypeStruct((b * s * SUBLANES, LANES), F32),
            jax.ShapeDtypeStruct((b, s, LANES), F32),
        ],
        compiler_params=pltpu.CompilerParams(
            dimension_semantics=("parallel", "parallel"), vmem_limit_bytes=VMEM_LIMIT),
        name="out_proj_router",
    )(a, z, x, w_out_bf16, ln_g, ln_b, w_route)


ISSUE_UNROLL = 8


def _moe_body(be_ref, src_ref, nv_ref, nused_ref, grow_ref, srow_ref, h8_hbm, wg_ref, wu_ref,
              wd_ref, out_hbm, xbuf, ybuf, wgb, wub, wdb, gsem, ssem, *, bm, n_tok):
    i = pl.program_id(0)
    n_used = nused_ref[0]
    n_pairs = 2 * n_tok
    tile = SUBLANES

    def tile_row(r):
        return r * tile if isinstance(r, int) else pl.multiple_of(r * tile, tile)

    def gather_copy(blk, buf_slot, r):
        src_row = grow_ref[src_ref[blk] + r]
        return pltpu.make_async_copy(
            h8_hbm.at[pl.ds(pl.multiple_of(src_row, tile), tile)],
            xbuf.at[buf_slot, pl.ds(tile_row(r), tile)],
            gsem.at[buf_slot])

    def scatter_copy(blk, buf_slot, r):
        blk_c = jnp.maximum(blk, 0)
        n_valid = jnp.where(blk >= 0, nv_ref[blk_c], 0)
        real_row = srow_ref[src_ref[blk_c] + r]
        pad_row = (n_pairs + buf_slot * bm + r) * tile
        dst_row = jnp.where(r < n_valid, real_row, pad_row)
        return pltpu.make_async_copy(
            ybuf.at[buf_slot, pl.ds(tile_row(r), tile)],
            out_hbm.at[pl.ds(pl.multiple_of(dst_row, tile), tile)],
            ssem.at[buf_slot])

    def issue_loop(make_copy, blk, buf_slot):
        def trip(g, c):
            for q in range(ISSUE_UNROLL):
                make_copy(blk, buf_slot, pl.multiple_of(g * ISSUE_UNROLL, ISSUE_UNROLL) + q).start()
            return c
        lax.fori_loop(0, bm // ISSUE_UNROLL, trip, 0)

    def wait_gather(buf_slot):
        pltpu.make_async_copy(h8_hbm.at[pl.ds(0, bm * tile)], xbuf.at[buf_slot],
                              gsem.at[buf_slot]).wait()

    def wait_scatter(buf_slot):
        pltpu.make_async_copy(ybuf.at[buf_slot], out_hbm.at[pl.ds(0, bm * tile)],
                              ssem.at[buf_slot]).wait()

    @pl.when(i == 0)
    def _():
        issue_loop(gather_copy, 0, 0)
        ybuf[...] = jnp.zeros_like(ybuf)
        for s in range(2):
            pad_rows = out_hbm.at[pl.ds((n_pairs + s * bm) * tile, bm * tile)]
            pltpu.make_async_copy(ybuf.at[s], pad_rows, ssem.at[s]).start()
            pltpu.make_async_copy(ybuf.at[s], pad_rows, ssem.at[s]).wait()

    def block_step(slot):
        other = 1 - slot
        wait_gather(slot)

        @pl.when(i >= 1)
        def _():
            wait_scatter(slot)

        prev = be_ref[jnp.maximum(i - 1, 0)]

        @pl.when((i == 0) | (be_ref[i] != prev))
        def _():
            wgb[...] = wg_ref[0].astype(BF16)
            wub[...] = wu_ref[0].astype(BF16)
            wdb[...] = wd_ref[0].astype(BF16)

        x = _load_token_tiles(xbuf.at[slot], bm).astype(BF16)
        nxt = jnp.minimum(i + 1, n_used - 1)
        for r in range(bm):
            gather_copy(nxt, other, r).start()
            scatter_copy(i - 1, other, r).start()
        gate = jnp.dot(x, wgb[...], preferred_element_type=F32)
        up = jnp.dot(x, wub[...], preferred_element_type=F32)
        hid = (jax.nn.silu(gate) * up).astype(BF16)
        y = jnp.dot(hid, wdb[...], preferred_element_type=F32)
        _store_token_tiles(ybuf.at[slot], y)

    for s in range(2):
        pl.when((i < n_used) & (i % 2 == s))(functools.partial(block_step, s))

    @pl.when(i == n_used)
    def _():
        last_slot = (n_used - 1) % 2
        issue_loop(scatter_copy, n_used - 1, last_slot)
        wait_scatter(0)
        wait_scatter(1)
        pltpu.make_async_copy(h8_hbm.at[pl.ds(0, bm * tile)], xbuf.at[1 - last_slot],
                              gsem.at[1 - last_slot]).wait()


def _moe_ffn(block_e, block_src, block_nv, n_used, gather_row, scatter_row, h8, w_gate, w_up,
             w_down, *, bm):
    n_tok = h8.shape[0] // SUBLANES
    d = SUBLANES * LANES
    n_blocks = block_e.shape[0]
    de = w_gate.shape[2]
    out_rows = (2 * n_tok + 2 * bm) * SUBLANES
    wspec = lambda shape: pl.BlockSpec(shape, lambda i, be, *_: (be[i], 0, 0))
    grid_spec = pltpu.PrefetchScalarGridSpec(
        num_scalar_prefetch=6,
        grid=(n_blocks,),
        in_specs=[
            pl.BlockSpec(memory_space=pl.ANY),
            wspec((1, d, de)), wspec((1, d, de)), wspec((1, de, d)),
        ],
        out_specs=pl.BlockSpec(memory_space=pl.ANY),
        scratch_shapes=[
            pltpu.VMEM((2, bm * SUBLANES, LANES), F32),
            pltpu.VMEM((2, bm * SUBLANES, LANES), F32),
            pltpu.VMEM((d, de), BF16),
            pltpu.VMEM((d, de), BF16),
            pltpu.VMEM((de, d), BF16),
            pltpu.SemaphoreType.DMA((2,)),
            pltpu.SemaphoreType.DMA((2,)),
        ],
    )
    return pl.pallas_call(
        functools.partial(_moe_body, bm=bm, n_tok=n_tok),
        grid_spec=grid_spec,
        out_shape=jax.ShapeDtypeStruct((out_rows, LANES), F32),
        compiler_params=pltpu.CompilerParams(
            dimension_semantics=("arbitrary",), vmem_limit_bytes=VMEM_LIMIT),
        name="moe_ffn",
    )(block_e, block_src, block_nv, n_used, gather_row, scatter_row, h8, w_gate, w_up, w_down)


def _combine_body(h8_ref, y0_ref, y1_ref, route_ref, g_ref, b_ref, o_ref, *, tt, alpha):
    route = route_ref[...]
    w0 = route[:, 2:3]
    w1 = route[:, 3:4]
    h = _load_token_tiles(h8_ref, tt)
    moe = w0 * _load_token_tiles(y0_ref, tt) + w1 * _load_token_tiles(y1_ref, tt)
    o_ref[...] = _layernorm(alpha * h + moe, g_ref[...], b_ref[...])


def _combine(h8, out2, route, ln_g, ln_b, *, alpha, tt):
    t = route.shape[0]
    d = SUBLANES * LANES
    n_tiles = t // tt
    tile_rows = pl.BlockSpec((tt * SUBLANES, LANES), lambda i: (i, 0))
    return pl.pallas_call(
        functools.partial(_combine_body, tt=tt, alpha=alpha),
        grid=(n_tiles,),
        in_specs=[
            tile_rows,
            tile_rows,
            pl.BlockSpec((tt * SUBLANES, LANES), lambda i: (n_tiles + i, 0)),
            pl.BlockSpec((tt, LANES), lambda i: (i, 0)),
            pl.BlockSpec((1, d), lambda i: (0, 0)),
            pl.BlockSpec((1, d), lambda i: (0, 0)),
        ],
        out_specs=pl.BlockSpec((tt, d), lambda i: (i, 0)),
        out_shape=jax.ShapeDtypeStruct((t, d), F32),
        compiler_params=pltpu.CompilerParams(
            dimension_semantics=("parallel",), vmem_limit_bytes=VMEM_LIMIT),
        name="moe_combine",
    )(h8, out2, out2, route, ln_g, ln_b)


def _moe_layout(route, *, bm):
    t = route.shape[0]
    tk = 2 * t
    flat_e = route[:, 0:2].astype(I32).reshape(-1)
    key = jnp.sort(flat_e * tk + jnp.arange(tk, dtype=I32))
    order = key % tk
    experts = jnp.arange(N_EXPERTS, dtype=I32)
    counts = jnp.sum((experts[:, None] == flat_e[None, :]).astype(I32), axis=1)
    starts = jnp.cumsum(counts) - counts
    pcounts = (counts + bm - 1) // bm * bm
    pends = jnp.cumsum(pcounts)
    pstarts = pends - pcounts
    n_blocks = (tk + N_EXPERTS * bm) // bm
    row0 = jnp.arange(n_blocks, dtype=I32) * bm
    block_e = jnp.minimum(
        jnp.sum((pends[None, :] <= row0[:, None]).astype(I32), axis=1), N_EXPERTS - 1)
    pick = (block_e[:, None] == experts[None, :]).astype(I32)
    lookup = lambda table: jnp.sum(pick * table[None, :], axis=1)
    off = row0 - lookup(pstarts)
    block_src = lookup(starts) + off
    block_nv = jnp.clip(lookup(counts) - off, 0, bm)
    n_used = (pends[-1] // bm).astype(I32).reshape(1)
    spare = jnp.zeros((bm,), I32)
    gather_row = jnp.concatenate([(order // 2) * SUBLANES, spare])
    scatter_row = jnp.concatenate([((order % 2) * t + order // 2) * SUBLANES, spare])
    return (block_e, block_src.astype(I32), block_nv.astype(I32), n_used, gather_row.astype(I32),
            scatter_row.astype(I32))


def _layer(x, w_in, lam_q1, lam_k1, lam_q2, lam_k2, subln_g, a_re, a_im, log_dt, b_re, b_im,
           c_re, c_im, d, w_glu, b_glu, w_out, ln1_g, ln1_b, w_rg, w_re, w_gate, w_up, w_down,
           ln2_g, ln2_b, *, layer_idx, alpha, ts, qb, tb, bm, tt):
    b, s, dm = x.shape
    lam_init = 0.8 - 0.6 * math.exp(-0.3 * layer_idx)

    q, k, v, u = _in_proj(x, w_in.astype(BF16), ts=ts)

    lam_params = jnp.stack([lam_q1, lam_k1, lam_q2, lam_k2]).astype(F32)
    a_out = _diff_attention(q, k, v, lam_params, subln_g.reshape(1, -1).astype(F32),
                            lam_init=lam_init, qb=qb)

    bbd, cbd, lre, lim = _ssm_params(a_re, a_im, log_dt, b_re, b_im, c_re, c_im)
    s_out = _ssm_glu(u, bbd, cbd, lre, lim, d.reshape(1, -1).astype(F32), w_glu.astype(BF16),
                     b_glu.reshape(1, -1).astype(F32), tb=tb)

    w_r = jnp.concatenate([w_rg, w_re], axis=1).astype(F32)
    w_r_hi = w_r.astype(BF16)
    w_r_lo = (w_r - w_r_hi.astype(F32)).astype(BF16)
    w_route = jnp.concatenate(
        [w_r_hi, w_r_lo, jnp.zeros((dm, LANES - 2 * N_ROUTE_COLS), BF16)], axis=1)
    h8, route = _out_proj(a_out, s_out, x, w_out.astype(BF16), ln1_g.reshape(1, -1),
                          ln1_b.reshape(1, -1), w_route, alpha=alpha, ts=ts)

    route = route.reshape(b * s, LANES)
    out2 = _moe_ffn(*_moe_layout(route, bm=bm), h8, w_gate, w_up, w_down, bm=bm)
    out = _combine(h8, out2, route, ln2_g.reshape(1, -1), ln2_b.reshape(1, -1), alpha=alpha, tt=tt)
    return out.reshape(b, s, dm)


def kernel(x, w_in, lam_q1, lam_k1, lam_q2, lam_k2, subln_g, ssm_a_re, ssm_a_im, ssm_log_dt, ssm_b_re, ssm_b_im, ssm_c_re, ssm_c_im, ssm_d, w_glu, b_glu, w_out, ln1_g, ln1_b, w_router_group, w_router_expert, w_exp_gate, w_exp_up, w_exp_down, ln2_g, ln2_b):
    depth = w_in.shape[0]
    alpha = (2.0 * depth) ** 0.25
    for i in range(depth):
        x = _layer(
            x, w_in[i], lam_q1[i], lam_k1[i], lam_q2[i], lam_k2[i], subln_g[i], ssm_a_re[i],
            ssm_a_im[i], ssm_log_dt[i], ssm_b_re[i], ssm_b_im[i], ssm_c_re[i], ssm_c_im[i],
            ssm_d[i], w_glu[i], b_glu[i], w_out[i], ln1_g[i], ln1_b[i], w_router_group[i],
            w_router_expert[i], w_exp_gate[i], w_exp_up[i], w_exp_down[i], ln2_g[i], ln2_b[i],
            layer_idx=i, alpha=alpha, ts=1024, qb=256, tb=256, bm=256, tt=512)
    return x
```
